```python
import math
import jax, jax.numpy as jnp
from jax import lax
import numpy as np

D_MODEL = 4096
BATCH = 4
SEQ = 4096
DEPTH = 2
DEC_BATCH = 8
DEC_SEQ = 64
PAST_LEN = 1024

CHUNK = 64
PE_DIM = 256
D_FF = ((8 * D_MODEL // 3 + 255) // 256) * 256
EPS = 1e-6

SSD_HEAD_DIM = 64
SSD_INNER = D_MODEL
SSD_HEADS = SSD_INNER // SSD_HEAD_DIM
SSD_GROUPS = 8
SSD_STATE = 128
SSD_CONV = 4
SSD_CONV_DIM = SSD_INNER + 2 * SSD_GROUPS * SSD_STATE

S5_WIDTH = D_MODEL
S5_GROUP = 16
S5_GROUPS = S5_WIDTH // S5_GROUP
S5_STATE = 64

SC_WIDTH = D_MODEL
SC_CONV = 3

N_SSM_LAYERS = (DEPTH + 1) // 2
N_CONV_LAYERS = DEPTH // 2
IN0 = SSD_INNER + SSD_CONV_DIM + SSD_HEADS + S5_WIDTH
SPLIT0 = (SSD_INNER, SSD_INNER + SSD_CONV_DIM, SSD_INNER + SSD_CONV_DIM + SSD_HEADS)

kernel_name = "hybrid_ssd_s5_shortconv_stream_step"


def rms_norm(x, g):
    xf = x.astype(jnp.float32)
    y = xf * lax.rsqrt(jnp.mean(xf * xf, axis=-1, keepdims=True) + EPS)
    return (y * g.astype(jnp.float32)).astype(x.dtype)


def swiglu(x, w_gate, w_up, w_down):
    return (jax.nn.silu(x @ w_gate) * (x @ w_up)) @ w_down


def causal_dwconv(x, hist, w):
    k, l = w.shape[0], x.shape[1]
    xp = jnp.concatenate([hist.astype(x.dtype), x], axis=1)
    y = sum(xp[:, j:j + l] * w[j] for j in range(k))
    return y, xp[:, l:]


def ssd_scan(x, dt, a, bm, cm, h0):
    b, l, h, p = x.shape
    g, n = bm.shape[2], bm.shape[3]
    r = h // g
    q = CHUNK if l % CHUNK == 0 else l
    nc = l // q
    f32 = jnp.float32

    def blocks(t):
        t = t.astype(f32).reshape((b, nc, q) + t.shape[2:])
        return jnp.moveaxis(t, 1, 0)

    xs = (blocks(x.reshape(b, l, g, r, p)), blocks(dt.reshape(b, l, g, r)), blocks(bm), blocks(cm))
    a = a.astype(f32).reshape(g, r)
    causal = jnp.tril(jnp.ones((q, q), bool))[None, :, :, None, None]

    def step(state, blk):
        xc, dtc, bc, cc = blk
        acum = jnp.cumsum(dtc * a, axis=1)
        seg = acum[:, :, None] - acum[:, None, :]
        decay = jnp.exp(jnp.where(causal, seg, -jnp.inf))
        scores = jnp.einsum('bign,bjgn->bijg', cc, bc)
        y_diag = jnp.einsum('bijg,bijgr,bjgr,bjgrp->bigrp', scores, decay, dtc, xc)
        y_off = jnp.einsum('bign,bgrpn,bigr->bigrp', cc, state, jnp.exp(acum))
        w_end = jnp.exp(acum[:, -1:] - acum) * dtc
        new_state = state * jnp.exp(acum[:, -1])[..., None, None] + jnp.einsum('bjgn,bjgr,bjgrp->bgrpn', bc, w_end, xc)
        return new_state, y_diag + y_off

    h_fin, ys = lax.scan(step, h0.astype(f32).reshape(b, g, r, p, n), xs)
    y = jnp.moveaxis(ys, 0, 1).reshape(b, l, h, p)
    return y, h_fin.reshape(b, h, p, n)


def ssd_mixer(z, xbc, dt_raw, conv_hist, h0, conv_w, conv_b, a_log, dt_bias, d_skip, norm_g):
    b, l, _ = z.shape
    f32 = jnp.float32
    xbc, conv_new = causal_dwconv(xbc, conv_hist, conv_w)
    xbc = jax.nn.silu(xbc + conv_b)
    xs, bm, cm = jnp.split(xbc, [SSD_INNER, SSD_INNER + SSD_GROUPS * SSD_STATE], axis=-1)
    xs = xs.reshape(b, l, SSD_HEADS, SSD_HEAD_DIM)
    bm = bm.reshape(b, l, SSD_GROUPS, SSD_STATE)
    cm = cm.reshape(b, l, SSD_GROUPS, SSD_STATE)
    dt = jax.nn.softplus((dt_raw + dt_bias).astype(f32))
    a = -jnp.exp(a_log.astype(f32))
    y, h_fin = ssd_scan(xs, dt, a, bm, cm, h0)
    y = y + d_skip.astype(f32)[:, None] * xs.astype(f32)
    y = y.reshape(b, l, SSD_INNER).astype(z.dtype) * jax.nn.silu(z)
    y = rms_norm(y.reshape(b, l, SSD_GROUPS, SSD_INNER // SSD_GROUPS),
                 norm_g.reshape(SSD_GROUPS, SSD_INNER // SSD_GROUPS)).reshape(b, l, SSD_INNER)
    return y, conv_new, h_fin.astype(h0.dtype)


def s5_mixer(u, h0_re, h0_im, lam_re, lam_im, log_step, b_re, b_im, c_re, c_im, d_skip, w_glu, b_glu):
    bsz, l, _ = u.shape
    f32 = jnp.float32
    lam = lax.complex(lam_re.astype(f32), lam_im.astype(f32))
    step = jnp.exp(log_step.astype(f32))[:, None]
    lam_bar = jnp.exp(lam * step)
    b_bar = ((lam_bar - 1.0) / lam)[..., None] * lax.complex(b_re.astype(f32), b_im.astype(f32))
    cmat = lax.complex(c_re.astype(f32), c_im.astype(f32))
    q = CHUNK if l % CHUNK == 0 else l
    nc = l // q
    ub = jnp.moveaxis(u.astype(f32).reshape(bsz, nc, q, S5_GROUPS, S5_GROUP), 1, 0)

    def combine(e1, e2):
        a1, x1 = e1
        a2, x2 = e2
        return a1 * a2, a2 * x1 + x2

    def block(state, uc):
        bu = jnp.einsum('gpk,bqgk->bqgp', b_bar, uc.astype(jnp.complex64))
        bu = bu.at[:, 0].add(lam_bar * state)
        _, hs = lax.associative_scan(combine, (jnp.broadcast_to(lam_bar, bu.shape), bu), axis=1)
        y = jnp.einsum('gkp,bqgp->bqgk', cmat, hs).real
        return hs[:, -1], y

    h0 = lax.complex(h0_re.astype(f32), h0_im.astype(f32))
    h_fin, ys = lax.scan(block, h0, ub)
    y = jnp.moveaxis(ys, 0, 1).reshape(bsz, l, S5_WIDTH) + d_skip.astype(f32) * u.astype(f32)
    y = jax.nn.gelu(y).astype(u.dtype)
    y = y * jax.nn.sigmoid(y @ w_glu + b_glu)
    return y, h_fin.real.astype(h0_re.dtype), h_fin.imag.astype(h0_im.dtype)


def short_conv_mixer(xn, hist, w_in, conv_w, w_out):
    bg, cg, v = jnp.split(xn @ w_in, 3, axis=-1)
    yc, new_hist = causal_dwconv(cg * v, hist, conv_w)
    return (bg * yc) @ w_out, new_hist


def run_trunk(x, p, conv_ssd, h_ssd, s5_re, s5_im, conv_sc, W):
    out_conv_ssd, out_h, out_re, out_im, out_sc = [], [], [], [], []
    for i in range(DEPTH):
        x = x + 0.5 * swiglu(rms_norm(x, W['g_ffn1'][i]), W['w_ffn1_gate'][i], W['w_ffn1_up'][i], W['w_ffn1_down'][i])
        xn = rms_norm(x, W['g_mix'][i])
        j = i // 2
        if i % 2 == 0:
            z, xbc, dt_raw, u = jnp.split(xn @ W['w_in0'][j], SPLIT0, axis=-1)
            ya, c_new, h_new = ssd_mixer(z, xbc, dt_raw, conv_ssd[j], h_ssd[j], W['ssd_conv_w'][j], W['ssd_conv_b'][j],
                                         W['ssd_a_log'][j], W['ssd_dt_bias'][j], W['ssd_d'][j], W['ssd_norm_g'][j])
            yb, re_new, im_new = s5_mixer(u, s5_re[j], s5_im[j], W['s5_lam_re'][j], W['s5_lam_im'][j], W['s5_log_step'][j],
                                          W['s5_b_re'][j], W['s5_b_im'][j], W['s5_c_re'][j], W['s5_c_im'][j],
                                          W['s5_d'][j], W['s5_w_glu'][j], W['s5_b_glu'][j])
            y = jnp.concatenate([ya, yb], axis=-1) @ W['w_out0'][j]
            out_conv_ssd.append(c_new)
            out_h.append(h_new)
            out_re.append(re_new)
            out_im.append(im_new)
        else:
            y, sc_new = short_conv_mixer(xn, conv_sc[j], W['w_in1'][j], W['sc_conv_w'][j], W['w_out1'][j])
            out_sc.append(sc_new)
        x = x + y
        x = x + 0.5 * swiglu(rms_norm(x, W['g_ffn2'][i]), W['w_ffn2_gate'][i], W['w_ffn2_up'][i], W['w_ffn2_down'][i])
        gate = jax.nn.sigmoid(rms_norm(x, W['g_pe'][i]) @ W['w_pe_gate'][i])
        x = x + gate * (p[i].astype(x.dtype) @ W['w_pe_proj'][i])
    return (rms_norm(x, W['g_final']), jnp.stack(out_conv_ssd), jnp.stack(out_h), jnp.stack(out_re),
            jnp.stack(out_im), jnp.stack(out_sc))


def setup_inputs(seed: int = 0) -> dict:
    key = jax.random.key(seed)
    ks = iter(jax.random.split(key, 64))
    f32 = jnp.float32

    def nrm(shape, scale):
        return jax.random.normal(next(ks), shape, f32) * scale

    def gain(shape):
        return 1.0 + nrm(shape, 0.02)

    L, M = N_SSM_LAYERS, N_CONV_LAYERS
    dt0 = jnp.exp(jax.random.uniform(next(ks), (L, SSD_HEADS), f32, math.log(1e-3), math.log(1e-1)))
    n_idx = jnp.arange(S5_STATE, dtype=f32)
    return {
        'x_prompt': nrm((BATCH, SEQ, D_MODEL), 1.0),
        'x_sample': nrm((DEC_BATCH, DEC_SEQ, D_MODEL), 1.0),
        'p_prompt': nrm((DEPTH, BATCH, SEQ, PE_DIM), 1.0),
        'p_sample': nrm((DEPTH, DEC_BATCH, DEC_SEQ, PE_DIM), 1.0),
        'cache_conv_ssd': nrm((L, DEC_BATCH, SSD_CONV - 1, SSD_CONV_DIM), 1.0),
        'state_ssd': nrm((L, DEC_BATCH, SSD_HEADS, SSD_HEAD_DIM, SSD_STATE), 0.1),
        'state_s5_re': nrm((L, DEC_BATCH, S5_GROUPS, S5_STATE), 0.1),
        'state_s5_im': nrm((L, DEC_BATCH, S5_GROUPS, S5_STATE), 0.1),
        'cache_conv_short': nrm((M, DEC_BATCH, SC_CONV - 1, SC_WIDTH), 1.0),
        'g_ffn1': gain((DEPTH, D_MODEL)),
        'w_ffn1_gate': nrm((DEPTH, D_MODEL, D_FF), D_MODEL ** -0.5),
        'w_ffn1_up': nrm((DEPTH, D_MODEL, D_FF), D_MODEL ** -0.5),
        'w_ffn1_down': nrm((DEPTH, D_FF, D_MODEL), D_FF ** -0.5),
        'g_mix': gain((DEPTH, D_MODEL)),
        'g_ffn2': gain((DEPTH, D_MODEL)),
        'w_ffn2_gate': nrm((DEPTH, D_MODEL, D_FF), D_MODEL ** -0.5),
        'w_ffn2_up': nrm((DEPTH, D_MODEL, D_FF), D_MODEL ** -0.5),
        'w_ffn2_down': nrm((DEPTH, D_FF, D_MODEL), D_FF ** -0.5),
        'g_pe': gain((DEPTH, D_MODEL)),
        'w_pe_gate': nrm((DEPTH, D_MODEL, D_MODEL), D_MODEL ** -0.5),
        'w_pe_proj': nrm((DEPTH, PE_DIM, D_MODEL), PE_DIM ** -0.5),
        'w_in0': nrm((L, D_MODEL, IN0), D_MODEL ** -0.5),
        'ssd_conv_w': nrm((L, SSD_CONV, SSD_CONV_DIM), SSD_CONV ** -0.5),
        'ssd_conv_b': nrm((L, SSD_CONV_DIM), 0.02),
        'ssd_a_log': jnp.log(jax.random.uniform(next(ks), (L, SSD_HEADS), f32, 1.0, 16.0)),
        'ssd_dt_bias': dt0 + jnp.log(-jnp.expm1(-dt0)),
        'ssd_d': gain((L, SSD_HEADS)),
        'ssd_norm_g': gain((L, SSD_INNER)),
        's5_lam_re': -0.5 + nrm((L, S5_GROUPS, S5_STATE), 0.01),
        's5_lam_im': math.pi * n_idx + nrm((L, S5_GROUPS, S5_STATE), 0.01),
        's5_log_step': jax.random.uniform(next(ks), (L, S5_GROUPS), f32, math.log(1e-3), math.log(1e-1)),
        's5_b_re': nrm((L, S5_GROUPS, S5_STATE, S5_GROUP), (2 * S5_GROUP) ** -0.5),
        's5_b_im': nrm((L, S5_GROUPS, S5_STATE, S5_GROUP), (2 * S5_GROUP) ** -0.5),
        's5_c_re': nrm((L, S5_GROUPS, S5_GROUP, S5_STATE), (2 * S5_STATE) ** -0.5),
        's5_c_im': nrm((L, S5_GROUPS, S5_GROUP, S5_STATE), (2 * S5_STATE) ** -0.5),
        's5_d': nrm((L, S5_WIDTH), 1.0),
        's5_w_glu': nrm((L, S5_WIDTH, S5_WIDTH), S5_WIDTH ** -0.5),
        's5_b_glu': nrm((L, S5_WIDTH), 0.02),
        'w_out0': nrm((L, SSD_INNER + S5_WIDTH, D_MODEL), (SSD_INNER + S5_WIDTH) ** -0.5),
        'w_in1': nrm((M, D_MODEL, 3 * SC_WIDTH), D_MODEL ** -0.5),
        'sc_conv_w': nrm((M, SC_CONV, SC_WIDTH), SC_CONV ** -0.5),
        'w_out1': nrm((M, SC_WIDTH, D_MODEL), SC_WIDTH ** -0.5),
        'g_final': gain((D_MODEL,)),
    }


def reference(x_prompt, x_sample, p_prompt, p_sample, cache_conv_ssd, state_ssd, state_s5_re, state_s5_im,
              cache_conv_short, g_ffn1, w_ffn1_gate, w_ffn1_up, w_ffn1_down, g_mix, g_ffn2, w_ffn2_gate,
              w_ffn2_up, w_ffn2_down, g_pe, w_pe_gate, w_pe_proj, w_in0, ssd_conv_w, ssd_conv_b, ssd_a_log,
              ssd_dt_bias, ssd_d, ssd_norm_g, s5_lam_re, s5_lam_im, s5_log_step, s5_b_re, s5_b_im, s5_c_re,
              s5_c_im, s5_d, s5_w_glu, s5_b_glu, w_out0, w_in1, sc_conv_w, w_out1, g_final):
    W = dict(g_ffn1=g_ffn1, w_ffn1_gate=w_ffn1_gate, w_ffn1_up=w_ffn1_up, w_ffn1_down=w_ffn1_down,
             g_mix=g_mix, g_ffn2=g_ffn2, w_ffn2_gate=w_ffn2_gate, w_ffn2_up=w_ffn2_up, w_ffn2_down=w_ffn2_down,
             g_pe=g_pe, w_pe_gate=w_pe_gate, w_pe_proj=w_pe_proj, w_in0=w_in0, ssd_conv_w=ssd_conv_w,
             ssd_conv_b=ssd_conv_b, ssd_a_log=ssd_a_log, ssd_dt_bias=ssd_dt_bias, ssd_d=ssd_d,
             ssd_norm_g=ssd_norm_g, s5_lam_re=s5_lam_re, s5_lam_im=s5_lam_im, s5_log_step=s5_log_step,
             s5_b_re=s5_b_re, s5_b_im=s5_b_im, s5_c_re=s5_c_re, s5_c_im=s5_c_im, s5_d=s5_d,
             s5_w_glu=s5_w_glu, s5_b_glu=s5_b_glu, w_out0=w_out0, w_in1=w_in1, sc_conv_w=sc_conv_w,
             w_out1=w_out1, g_final=g_final)
    b, dt = x_prompt.shape[0], x_prompt.dtype
    z_conv_ssd = jnp.zeros((N_SSM_LAYERS, b, SSD_CONV - 1, SSD_CONV_DIM), dt)
    z_ssd = jnp.zeros((N_SSM_LAYERS, b, SSD_HEADS, SSD_HEAD_DIM, SSD_STATE), dt)
    z_s5 = jnp.zeros((N_SSM_LAYERS, b, S5_GROUPS, S5_STATE), dt)
    z_sc = jnp.zeros((N_CONV_LAYERS, b, SC_CONV - 1, SC_WIDTH), dt)
    y_prompt, pc_ssd, p_ssd, p_re, p_im, pc_sc = run_trunk(x_prompt, p_prompt, z_conv_ssd, z_ssd, z_s5, z_s5, z_sc, W)
    y_sample, sc_ssd, s_ssd, s_re, s_im, sc_sc = run_trunk(x_sample, p_sample, cache_conv_ssd, state_ssd,
                                                          state_s5_re, state_s5_im, cache_conv_short, W)
    return (y_prompt, y_sample, pc_ssd, p_ssd, p_re, p_im, pc_sc, sc_ssd, s_ssd, s_re, s_im, sc_sc)
```

```python
import functools

import jax
import jax.numpy as jnp
from jax import lax
from jax.experimental import pallas as pl
from jax.experimental.pallas import tpu as pltpu

F32 = jnp.float32
BF16 = jnp.bfloat16
EPS = 1e-6
SSD_CHUNK = 64
S5_GROUP_BLOCK = 16
LANES = 128
SUBLANES = 8
V7X_VMEM_BYTES = 64 * 1024 * 1024
MM_VMEM_BUDGET = 40 * 1024 * 1024
HIGHEST = lax.Precision.HIGHEST
NT_DIMS = (((1,), (1,)), ((), ()))
TN_DIMS = (((0,), (0,)), ((), ()))


def _largest_divisor(n, candidates):
    for c in candidates:
        if n % c == 0:
            return c
    raise ValueError(f"no tile in {candidates} divides {n}")


def _round_up(n, m):
    return (n + m - 1) // m * m


def _params(semantics, vmem_estimate):
    limit = min(V7X_VMEM_BYTES - 8 * 1024 * 1024, int(vmem_estimate * 1.25) + 8 * 1024 * 1024)
    return pltpu.CompilerParams(dimension_semantics=semantics, vmem_limit_bytes=limit)


def _silu(x):
    return x * jax.nn.sigmoid(x)


def _rmsnorm_body(x_ref, g_ref, o_ref):
    x = x_ref[...]
    inv = lax.rsqrt(jnp.mean(x * x, axis=-1, keepdims=True) + EPS)
    o_ref[...] = (x * inv * g_ref[...]).astype(o_ref.dtype)


def _rmsnorm(x, g, out_dtype):
    m, d = x.shape
    tm = _largest_divisor(m, (256, 128, 64, 32, 16, 8))
    return pl.pallas_call(
        _rmsnorm_body,
        out_shape=jax.ShapeDtypeStruct((m, d), out_dtype),
        grid=(m // tm,),
        in_specs=[pl.BlockSpec((tm, d), lambda i: (i, 0)), pl.BlockSpec((1, d), lambda i: (0, 0))],
        out_specs=pl.BlockSpec((tm, d), lambda i: (i, 0)),
        compiler_params=_params(("parallel",), 2 * tm * d * (4 + jnp.dtype(out_dtype).itemsize) + 3 * tm * d * 4),
        name="rmsnorm",
    )(x, g.reshape(1, d))


def _mm_body(*refs, n_a, n_w, dots, epilogue):
    a_refs, w_refs, e_refs, o_ref = refs[:n_a], refs[n_a:n_a + n_w], refs[n_a + n_w:-1], refs[-1]
    a_vals = {}
    accs = []
    for ai, wi in dots:
        if ai not in a_vals:
            a_vals[ai] = a_refs[ai][...].astype(BF16)
        accs.append(jnp.dot(a_vals[ai], w_refs[wi][...], preferred_element_type=F32))
    o_ref[...] = epilogue(accs, [e[...] for e in e_refs]).astype(o_ref.dtype)


def _mm_vmem(tm, tn, a_list, w_list, n_dots, tiles, out_dtype):
    a_bytes = sum(tm * a.shape[1] * a.dtype.itemsize for a in a_list)
    w_bytes = sum(w.shape[0] * tn * w.dtype.itemsize for w in w_list)
    io_bytes = tm * tn * (jnp.dtype(out_dtype).itemsize + 4 * tiles)
    return 2 * (a_bytes + w_bytes + io_bytes) + (n_dots + 1) * tm * tn * 4


def _fused_matmul(a_list, w_list, dots, extras, epilogue, out_dtype, name):
    m, n = a_list[0].shape[0], w_list[0].shape[1]
    n_tiles = sum(1 for e in extras if e.shape[0] != 1)
    choice = None
    for tm in (1024, 512, 256, 128, 64, 32, 16, 8):
        if m % tm:
            continue
        for tn in (1024, 512, 256, 128):
            if n % tn:
                continue
            est = _mm_vmem(tm, tn, a_list, w_list, len(dots), n_tiles, out_dtype)
            if est <= MM_VMEM_BUDGET:
                choice = (tm, tn, est)
                break
        if choice:
            break
    if choice is None:
        raise ValueError(f"{name}: no matmul tiling fits VMEM")
    tm, tn, est = choice
    in_specs = [pl.BlockSpec((tm, a.shape[1]), lambda i, j: (i, 0)) for a in a_list]
    in_specs += [pl.BlockSpec((w.shape[0], tn), lambda i, j: (0, j)) for w in w_list]
    for e in extras:
        if e.shape[0] == 1:
            in_specs.append(pl.BlockSpec((1, tn), lambda i, j: (0, j)))
        else:
            in_specs.append(pl.BlockSpec((tm, tn), lambda i, j: (i, j)))
    body = functools.partial(_mm_body, n_a=len(a_list), n_w=len(w_list), dots=tuple(dots), epilogue=epilogue)
    return pl.pallas_call(
        body,
        out_shape=jax.ShapeDtypeStruct((m, n), out_dtype),
        grid=(m // tm, n // tn),
        in_specs=in_specs,
        out_specs=pl.BlockSpec((tm, tn), lambda i, j: (i, j)),
        compiler_params=_params(("parallel", "parallel"), est),
        name=name,
    )(*a_list, *w_list, *extras)


def _ep_plain(accs, ex):
    return accs[0]


def _ep_swiglu(accs, ex):
    return _silu(accs[0]) * accs[1]


def _ep_half_residual(accs, ex):
    return ex[0] + 0.5 * accs[0]


def _ep_residual(accs, ex):
    return ex[0] + accs[0]


def _ep_residual2(accs, ex):
    return ex[0] + (accs[0] + accs[1])


def _ep_glu(accs, ex):
    return ex[0] * jax.nn.sigmoid(accs[0] + ex[1])


def _ep_gated_embed(accs, ex):
    return ex[0] + jax.nn.sigmoid(accs[0]) * accs[1]


def _ssd_body(z_ref, xs_ref, bm_ref, cm_ref, dtc_ref, dtr_ref, hx_ref, hb_ref, hc_ref, h0_ref,
              wx_ref, wb_ref, wc_ref, bx_ref, bb_ref, bc_ref, alc_ref, alr_ref, dbc_ref, dbr_ref, drep_ref, ng_ref,
              ya_ref, hfin_ref, state_sc, xp_sc, y_sc, *, heads, head_dim, n_state, chunk, conv_k):
    c = pl.program_id(2)
    q, p, n = chunk, head_dim, n_state
    w = heads * p
    t0 = SUBLANES - (conv_k - 1)

    @pl.when(c == 0)
    def _load_carried():
        state_sc[...] = h0_ref[...]
        xp_sc[t0:SUBLANES, 0:w] = hx_ref[...]
        xp_sc[t0:SUBLANES, w:w + n] = hb_ref[...]
        xp_sc[t0:SUBLANES, w + n:w + 2 * n] = hc_ref[...]

    xp_sc[SUBLANES:SUBLANES + q, 0:w] = xs_ref[...]
    xp_sc[SUBLANES:SUBLANES + q, w:w + n] = bm_ref[...]
    xp_sc[SUBLANES:SUBLANES + q, w + n:w + 2 * n] = cm_ref[...]

    def conv_silu(lo, hi, w_ref, b_ref):
        acc = xp_sc[t0:t0 + q, lo:hi] * w_ref[0:1, :]
        for j in range(1, conv_k):
            acc = acc + xp_sc[t0 + j:t0 + j + q, lo:hi] * w_ref[j:j + 1, :]
        return _silu(acc + b_ref[...])

    xs = conv_silu(0, w, wx_ref, bx_ref)
    bm = conv_silu(w, w + n, wb_ref, bb_ref)
    cm = conv_silu(w + n, w + 2 * n, wc_ref, bc_ref)
    xp_sc[t0:SUBLANES, :] = xp_sc[SUBLANES + q - (conv_k - 1):SUBLANES + q, :]

    dt_c = jax.nn.softplus(dtc_ref[...] + dbc_ref[...])
    dt_r = jax.nn.softplus(dtr_ref[...] + dbr_ref[...])
    a_c = -jnp.exp(alc_ref[...])
    a_r = -jnp.exp(alr_ref[...])
    ii = lax.broadcasted_iota(jnp.int32, (q, q), 0)
    jj = lax.broadcasted_iota(jnp.int32, (q, q), 1)
    causal = ii >= jj
    acum_c = jnp.dot(causal.astype(F32), dt_c * a_c, precision=HIGHEST, preferred_element_type=F32)
    acum_r = jnp.dot(dt_r * a_r, (ii <= jj).astype(F32), precision=HIGHEST, preferred_element_type=F32)
    alast_c = acum_c[q - 1:q, :]
    grow_c = jnp.exp(acum_c)
    wend_c = jnp.exp(alast_c - acum_c) * dt_c
    elast_c = jnp.exp(alast_c)

    cm_b = cm.astype(BF16)
    bm_b = bm.astype(BF16)
    scores = lax.dot_general(cm_b, bm_b, NT_DIMS, preferred_element_type=F32)
    for h in range(heads):
        seg = acum_c[:, h:h + 1] - acum_r[h:h + 1, :]
        decay = jnp.exp(jnp.where(causal, seg, -jnp.inf))
        m_h = (scores * decay * dt_r[h:h + 1, :]).astype(BF16)
        x_h = xs[:, h * p:(h + 1) * p]
        st = state_sc[h]
        y_off = lax.dot_general(cm_b, st.astype(BF16), NT_DIMS, preferred_element_type=F32)
        y_sc[:, h * p:(h + 1) * p] = (jnp.dot(m_h, x_h.astype(BF16), preferred_element_type=F32)
                                      + y_off * grow_c[:, h:h + 1])
        xw = (x_h * wend_c[:, h:h + 1]).astype(BF16)
        state_sc[h] = st * elast_c[:, h:h + 1] + lax.dot_general(xw, bm_b, TN_DIMS, preferred_element_type=F32)

    y = y_sc[...] + drep_ref[...] * xs
    y = y * _silu(z_ref[...])
    inv = lax.rsqrt(jnp.mean(y * y, axis=-1, keepdims=True) + EPS)
    ya_ref[...] = (y * inv * ng_ref[...]).astype(ya_ref.dtype)

    @pl.when(c == pl.num_programs(2) - 1)
    def _store_state():
        hfin_ref[...] = state_sc[...]


def _ssd_mixer(zx, dt_raw, conv_hist, h0, conv_w, conv_b, a_log, dt_bias, d_skip, norm_g):
    b, l, _ = zx.shape
    _, heads, p, n = h0.shape
    inner = heads * p
    conv_k, conv_dim = conv_w.shape
    g = (conv_dim - inner) // (2 * n)
    r = heads // g
    w = r * p
    q = SSD_CHUNK if l % SSD_CHUNK == 0 else l
    nc = l // q
    dt5 = dt_raw.reshape(b, nc, q, g, r)
    dtc = dt5.transpose(0, 3, 1, 2, 4)
    dtr = dt5.transpose(0, 3, 1, 4, 2)
    conv_b2 = conv_b.reshape(1, conv_dim)
    nb0 = inner // n

    def col(block_w, first):
        return pl.BlockSpec((None, q, block_w), lambda bi, gi, ci: (bi, ci, first + gi))

    def hist(block_w, first):
        return pl.BlockSpec((None, conv_k - 1, block_w), lambda bi, gi, ci: (bi, 0, first + gi))

    def wrow(rows, block_w, first):
        return pl.BlockSpec((rows, block_w), lambda bi, gi, ci: (0, first + gi))

    def per_group(shape):
        return pl.BlockSpec((None,) + shape, lambda bi, gi, ci: (gi, 0, 0))

    in_specs = [
        col(w, 0), col(w, inner // w), col(n, 2 * inner // n), col(n, 2 * inner // n + g),
        pl.BlockSpec((None, None, None, q, r), lambda bi, gi, ci: (bi, gi, ci, 0, 0)),
        pl.BlockSpec((None, None, None, r, q), lambda bi, gi, ci: (bi, gi, ci, 0, 0)),
        hist(w, 0), hist(n, nb0), hist(n, nb0 + g),
        pl.BlockSpec((None, r, p, n), lambda bi, gi, ci: (bi, gi, 0, 0)),
        wrow(conv_k, w, 0), wrow(conv_k, n, nb0), wrow(conv_k, n, nb0 + g),
        wrow(1, w, 0), wrow(1, n, nb0), wrow(1, n, nb0 + g),
        per_group((1, r)), per_group((r, 1)), per_group((1, r)), per_group((r, 1)),
        per_group((1, w)), per_group((1, w)),
    ]
    out_specs = [
        pl.BlockSpec((None, q, w), lambda bi, gi, ci: (bi, ci, gi)),
        pl.BlockSpec((None, r, p, n), lambda bi, gi, ci: (bi, gi, 0, 0)),
    ]
    est = 4 * (2 * q * (2 * w + 2 * n) + 6 * r * p * n + (SUBLANES + q) * (w + 2 * n) + 8 * q * w)
    body = functools.partial(_ssd_body, heads=r, head_dim=p, n_state=n, chunk=q, conv_k=conv_k)
    ya, h_fin = pl.pallas_call(
        body,
        out_shape=[jax.ShapeDtypeStruct((b, l, inner), BF16), jax.ShapeDtypeStruct(h0.shape, F32)],
        grid=(b, g, nc),
        in_specs=in_specs,
        out_specs=out_specs,
        scratch_shapes=[pltpu.VMEM((r, p, n), F32), pltpu.VMEM((SUBLANES + q, w + 2 * n), F32),
                        pltpu.VMEM((q, w), F32)],
        compiler_params=_params(("parallel", "parallel", "arbitrary"), est),
        name="ssd_mixer",
    )(zx, zx, zx, zx, dtc, dtr, conv_hist, conv_hist, conv_hist, h0,
      conv_w, conv_w, conv_w, conv_b2, conv_b2, conv_b2,
      a_log.reshape(g, 1, r), a_log.reshape(g, r, 1), dt_bias.reshape(g, 1, r), dt_bias.reshape(g, r, 1),
      jnp.repeat(d_skip, p).reshape(g, 1, w), norm_g.reshape(g, 1, w))
    return ya, h_fin


def _gelu_tanh(x):
    return 0.5 * x * (1.0 + jnp.tanh(0.7978845608028654 * (x + 0.044715 * (x * x * x))))


def _s5_body(u_ref, h0re_ref, h0im_ref, lam_ref, bblk_ref, cblk_ref, d_ref,
             y_ref, ybf_ref, hre_ref, him_ref, bu_sc, h_sc, *, n_seq, t_blk, n_tiles):
    j = pl.program_id(1)
    half = n_tiles // 2
    rows = n_seq * t_blk

    @pl.when(j == 0)
    def _load_state():
        for k in range(half):
            h_sc[k] = h0re_ref[:, k * LANES:(k + 1) * LANES]
            h_sc[half + k] = h0im_ref[:, k * LANES:(k + 1) * LANES]

    u = u_ref[...].reshape(rows, u_ref.shape[-1])
    u_b = u.astype(BF16)
    for kk in range(half):
        bu = jnp.dot(u_b, bblk_ref[:, kk * 2 * LANES:(kk + 1) * 2 * LANES], preferred_element_type=F32)
        bu_sc[2 * kk] = bu[:, :LANES]
        bu_sc[2 * kk + 1] = bu[:, LANES:]

    lam = lam_ref[...]
    lam_re = [jnp.broadcast_to(lam[:, k * LANES:(k + 1) * LANES], (n_seq, LANES)) for k in range(half)]
    lam_im = [jnp.broadcast_to(lam[:, (half + k) * LANES:(half + k + 1) * LANES], (n_seq, LANES)) for k in range(half)]

    def step(t, hs):
        idx = pl.ds(t, n_seq, stride=t_blk)
        new_re, new_im = [], []
        for k in range(half):
            h_re, h_im = hs[k], hs[half + k]
            re_rows, im_rows = bu_sc.at[k], bu_sc.at[half + k]
            n_re = lam_re[k] * h_re - lam_im[k] * h_im + re_rows[idx, :]
            n_im = lam_re[k] * h_im + lam_im[k] * h_re + im_rows[idx, :]
            re_rows[idx, :] = n_re
            im_rows[idx, :] = n_im
            new_re.append(n_re)
            new_im.append(n_im)
        return tuple(new_re + new_im)

    hs = lax.fori_loop(0, t_blk, step, tuple(h_sc[k] for k in range(n_tiles)))
    for k in range(n_tiles):
        h_sc[k] = hs[k]

    h_all = jnp.concatenate([bu_sc[k].astype(BF16) for k in range(n_tiles)], axis=1)
    y = jnp.dot(h_all, cblk_ref[...], preferred_element_type=F32) + d_ref[...] * u
    y = _gelu_tanh(y)
    y_ref[...] = y.reshape(y_ref.shape)
    ybf_ref[...] = y.astype(BF16).reshape(ybf_ref.shape)

    @pl.when(j == pl.num_programs(1) - 1)
    def _store_state():
        for k in range(half):
            hre_ref[:, k * LANES:(k + 1) * LANES] = hs[k]
            him_ref[:, k * LANES:(k + 1) * LANES] = hs[half + k]


def _s5_discretize(lam_re, lam_im, log_step, b_re, b_im, c_re, c_im):
    g, p, k = b_re.shape
    gb = S5_GROUP_BLOCK
    nb = g // gb
    step = jnp.exp(log_step)[:, None]
    mag = jnp.exp(lam_re * step)
    lb_re, lb_im = mag * jnp.cos(lam_im * step), mag * jnp.sin(lam_im * step)
    den = lam_re * lam_re + lam_im * lam_im
    f_re = ((lb_re - 1.0) * lam_re + lb_im * lam_im) / den
    f_im = (lb_im * lam_re - (lb_re - 1.0) * lam_im) / den
    bb_re = f_re[..., None] * b_re - f_im[..., None] * b_im
    bb_im = f_re[..., None] * b_im + f_im[..., None] * b_re
    eye = jnp.eye(gb, dtype=F32)
    lam = jnp.stack([lb_re.reshape(nb, gb * p), lb_im.reshape(nb, gb * p)], axis=1).reshape(nb, 1, 2 * gb * p)
    bb = jnp.stack([bb_re, bb_im]).reshape(2, nb, gb, p, k).transpose(1, 2, 4, 0, 3)
    bblk = (bb[:, :, :, :, None, :] * eye[None, :, None, None, :, None]).reshape(nb, gb * k, 2 * gb * p)
    cc = jnp.stack([c_re, -c_im]).reshape(2, nb, gb, k, p).transpose(1, 0, 2, 4, 3)
    cblk = (cc[:, :, :, :, None, :] * eye[None, None, :, None, :, None]).reshape(nb, 2 * gb * p, gb * k)
    return lam, bblk.astype(BF16), cblk.astype(BF16)


def _s5_mixer(u, h0_re, h0_im, lam, bblk, cblk, d_skip):
    b, l, width = u.shape
    _, g, p = h0_re.shape
    nb, cw, sw2 = bblk.shape
    n_tiles = sw2 // LANES
    t_blk = _largest_divisor(l, (256, 128, 64, 32, 16, 8))
    rows = b * t_blk
    est = 4 * (rows * sw2 + 2 * 3 * rows * cw + rows * sw2 // 2 + 4 * rows * cw) + 2 * 2 * 2 * cw * sw2
    body = functools.partial(_s5_body, n_seq=b, t_blk=t_blk, n_tiles=n_tiles)
    seq_spec = pl.BlockSpec((b, t_blk, cw), lambda gi, ti: (0, ti, gi))
    state_spec = pl.BlockSpec((b, sw2 // 2), lambda gi, ti: (0, gi))
    y, ybf, h_re, h_im = pl.pallas_call(
        body,
        out_shape=[jax.ShapeDtypeStruct((b, l, width), F32), jax.ShapeDtypeStruct((b, l, width), BF16),
                   jax.ShapeDtypeStruct((b, g * p), F32), jax.ShapeDtypeStruct((b, g * p), F32)],
        grid=(nb, l // t_blk),
        in_specs=[seq_spec, state_spec, state_spec,
                  pl.BlockSpec((None, 1, sw2), lambda gi, ti: (gi, 0, 0)),
                  pl.BlockSpec((None, cw, sw2), lambda gi, ti: (gi, 0, 0)),
                  pl.BlockSpec((None, sw2, cw), lambda gi, ti: (gi, 0, 0)),
                  pl.BlockSpec((1, cw), lambda gi, ti: (0, gi))],
        out_specs=[seq_spec, seq_spec, state_spec, state_spec],
        scratch_shapes=[pltpu.VMEM((n_tiles, rows, LANES), F32), pltpu.VMEM((n_tiles, b, LANES), F32)],
        compiler_params=_params(("parallel", "arbitrary"), est),
        name="s5_mixer",
    )(u, h0_re.reshape(b, g * p), h0_im.reshape(b, g * p), lam, bblk, cblk, d_skip.reshape(1, width))
    return y, ybf, h_re.reshape(b, g, p), h_im.reshape(b, g, p)


def _short_conv_body(b_ref, c_ref, v_ref, hist_ref, w_ref, o_ref, hnew_ref, cv_sc, *, t_blk, conv_k):
    t = pl.program_id(2)
    t0 = SUBLANES - (conv_k - 1)

    @pl.when(t == 0)
    def _load_hist():
        cv_sc[t0:SUBLANES, :] = hist_ref[...]

    cv_sc[SUBLANES:SUBLANES + t_blk, :] = c_ref[...] * v_ref[...]
    acc = cv_sc[t0:t0 + t_blk, :] * w_ref[0:1, :]
    for j in range(1, conv_k):
        acc = acc + cv_sc[t0 + j:t0 + j + t_blk, :] * w_ref[j:j + 1, :]
    o_ref[...] = (b_ref[...] * acc).astype(o_ref.dtype)
    cv_sc[t0:SUBLANES, :] = cv_sc[SUBLANES + t_blk - (conv_k - 1):SUBLANES + t_blk, :]

    @pl.when(t == pl.num_programs(2) - 1)
    def _store_hist():
        hnew_ref[...] = cv_sc[t0:SUBLANES, :]


def _short_conv(bcv, hist, conv_w):
    bsz, l, width3 = bcv.shape
    width = width3 // 3
    conv_k = conv_w.shape[0]
    t_blk = _largest_divisor(l, (256, 128, 64, 32, 16, 8))
    tw = _largest_divisor(width, (1024, 512, 256, 128))
    nw = width // tw

    def col(first):
        return pl.BlockSpec((None, t_blk, tw), lambda bi, wi, ti: (bi, ti, first + wi))

    body = functools.partial(_short_conv_body, t_blk=t_blk, conv_k=conv_k)
    return pl.pallas_call(
        body,
        out_shape=[jax.ShapeDtypeStruct((bsz, l, width), BF16), jax.ShapeDtypeStruct(hist.shape, F32)],
        grid=(bsz, nw, l // t_blk),
        in_specs=[col(0), col(nw), col(2 * nw),
                  pl.BlockSpec((None, conv_k - 1, tw), lambda bi, wi, ti: (bi, 0, wi)),
                  pl.BlockSpec((conv_k, tw), lambda bi, wi, ti: (0, wi))],
        out_specs=[pl.BlockSpec((None, t_blk, tw), lambda bi, wi, ti: (bi, ti, wi)),
                   pl.BlockSpec((None, conv_k - 1, tw), lambda bi, wi, ti: (bi, 0, wi))],
        scratch_shapes=[pltpu.VMEM((SUBLANES + t_blk, tw), F32)],
        compiler_params=_params(("parallel", "parallel", "arbitrary"), 4 * t_blk * tw * 12),
        name="short_conv",
    )(bcv, bcv, bcv, hist, conv_w)


def _prepare_weights(w):
    depth, d_model, d_ff = w["w_ffn1_gate"].shape
    ffp = _round_up(d_ff, 1024)
    heads, p, _ = w["ssd_shape"]
    inner = heads * p
    conv_dim = w["ssd_conv_w"].shape[-1]
    out = {}
    for name in ("w_ffn1_gate", "w_ffn1_up", "w_ffn2_gate", "w_ffn2_up"):
        out[name] = jnp.pad(w[name].astype(BF16), ((0, 0), (0, 0), (0, ffp - d_ff)))
    for name in ("w_ffn1_down", "w_ffn2_down"):
        out[name] = jnp.pad(w[name].astype(BF16), ((0, 0), (0, ffp - d_ff), (0, 0)))
    for name in ("w_pe_gate", "w_pe_proj", "s5_w_glu", "w_in1", "w_out1"):
        out[name] = w[name].astype(BF16)
    w_in0 = w["w_in0"]
    s1 = inner + conv_dim
    out["w_zx"] = w_in0[:, :, :s1].astype(BF16)
    out["w_dt"] = jnp.pad(w_in0[:, :, s1:s1 + heads].astype(BF16), ((0, 0), (0, 0), (0, _round_up(heads, LANES) - heads)))
    out["w_u"] = w_in0[:, :, s1 + heads:].astype(BF16)
    out["w_out0_a"] = w["w_out0"][:, :inner].astype(BF16)
    out["w_out0_b"] = w["w_out0"][:, inner:].astype(BF16)
    out["s5"] = [_s5_discretize(w["s5_lam_re"][j], w["s5_lam_im"][j], w["s5_log_step"][j], w["s5_b_re"][j],
                                w["s5_b_im"][j], w["s5_c_re"][j], w["s5_c_im"][j])
                 for j in range(w["s5_lam_re"].shape[0])]
    return out


def _ffn_half_step(x, g, w_gate, w_up, w_down, tag):
    xn = _rmsnorm(x, g, BF16)
    h = _fused_matmul([xn], [w_gate, w_up], [(0, 0), (0, 1)], [], _ep_swiglu, BF16, f"{tag}_gate_up")
    return _fused_matmul([h], [w_down], [(0, 0)], [x], _ep_half_residual, F32, f"{tag}_down")


def _run_trunk(x3, p4, conv_ssd, h_ssd, s5_re, s5_im, conv_sc, w, wb):
    bsz, l, d = x3.shape
    m = bsz * l
    depth = w["g_ffn1"].shape[0]
    heads = h_ssd.shape[2]
    x = x3.reshape(m, d)
    out_conv_ssd, out_h, out_re, out_im, out_sc = [], [], [], [], []
    for i in range(depth):
        x = _ffn_half_step(x, w["g_ffn1"][i], wb["w_ffn1_gate"][i], wb["w_ffn1_up"][i], wb["w_ffn1_down"][i], "ffn1")
        xn = _rmsnorm(x, w["g_mix"][i], BF16)
        j = i // 2
        if i % 2 == 0:
            zx = _fused_matmul([xn], [wb["w_zx"][j]], [(0, 0)], [], _ep_plain, F32, "in0_zxbc")
            dt = _fused_matmul([xn], [wb["w_dt"][j]], [(0, 0)], [], _ep_plain, F32, "in0_dt")[:, :heads]
            u = _fused_matmul([xn], [wb["w_u"][j]], [(0, 0)], [], _ep_plain, F32, "in0_u")
            zx3 = zx.reshape(bsz, l, zx.shape[-1])
            ya, h_new = _ssd_mixer(zx3, dt.reshape(bsz, l, heads), conv_ssd[j], h_ssd[j], w["ssd_conv_w"][j],
                                   w["ssd_conv_b"][j], w["ssd_a_log"][j], w["ssd_dt_bias"][j], w["ssd_d"][j],
                                   w["ssd_norm_g"][j])
            inner = ya.shape[-1]
            conv_k, conv_dim = w["ssd_conv_w"][j].shape
            xbc_tail = zx3[:, l - min(l, conv_k - 1):, inner:inner + conv_dim]
            c_new = jnp.concatenate([conv_ssd[j], xbc_tail], axis=1)[:, -(conv_k - 1):]
            lam, bblk, cblk = wb["s5"][j]
            yb, yb_bf, re_new, im_new = _s5_mixer(u.reshape(bsz, l, u.shape[-1]), s5_re[j], s5_im[j], lam, bblk, cblk,
                                                  w["s5_d"][j])
            width = yb.shape[-1]
            yb_glu = _fused_matmul([yb_bf.reshape(m, width)], [wb["s5_w_glu"][j]], [(0, 0)],
                                   [yb.reshape(m, width), w["s5_b_glu"][j].reshape(1, width)], _ep_glu, BF16, "s5_glu")
            x = _fused_matmul([ya.reshape(m, inner), yb_glu], [wb["w_out0_a"][j], wb["w_out0_b"][j]],
                              [(0, 0), (1, 1)], [x], _ep_residual2, F32, "out0")
            out_conv_ssd.append(c_new)
            out_h.append(h_new)
            out_re.append(re_new)
            out_im.append(im_new)
        else:
            bcv = _fused_matmul([xn], [wb["w_in1"][j]], [(0, 0)], [], _ep_plain, F32, "in1")
            yc, sc_new = _short_conv(bcv.reshape(bsz, l, bcv.shape[-1]), conv_sc[j], w["sc_conv_w"][j])
            x = _fused_matmul([yc.reshape(m, yc.shape[-1])], [wb["w_out1"][j]], [(0, 0)], [x], _ep_residual, F32, "out1")
            out_sc.append(sc_new)
        x = _ffn_half_step(x, w["g_ffn2"][i], wb["w_ffn2_gate"][i], wb["w_ffn2_up"][i], wb["w_ffn2_down"][i], "ffn2")
        xn = _rmsnorm(x, w["g_pe"][i], BF16)
        x = _fused_matmul([xn, p4[i].reshape(m, p4.shape[-1])], [wb["w_pe_gate"][i], wb["w_pe_proj"][i]],
                          [(0, 0), (1, 1)], [x], _ep_gated_embed, F32, "pe_embed")
    y = _rmsnorm(x, w["g_final"], F32).reshape(bsz, l, d)
    return (y, jnp.stack(out_conv_ssd), jnp.stack(out_h), jnp.stack(out_re), jnp.stack(out_im), jnp.stack(out_sc))


def kernel(x_prompt, x_sample, p_prompt, p_sample, cache_conv_ssd, state_ssd, state_s5_re, state_s5_im, cache_conv_short, g_ffn1, w_ffn1_gate, w_ffn1_up, w_ffn1_down, g_mix, g_ffn2, w_ffn2_gate, w_ffn2_up, w_ffn2_down, g_pe, w_pe_gate, w_pe_proj, w_in0, ssd_conv_w, ssd_conv_b, ssd_a_log, ssd_dt_bias, ssd_d, ssd_norm_g, s5_lam_re, s5_lam_im, s5_log_step, s5_b_re, s5_b_im, s5_c_re, s5_c_im, s5_d, s5_w_glu, s5_b_glu, w_out0, w_in1, sc_conv_w, w_out1, g_final):
    w = dict(g_ffn1=g_ffn1, w_ffn1_gate=w_ffn1_gate, w_ffn1_up=w_ffn1_up, w_ffn1_down=w_ffn1_down,
             g_mix=g_mix, g_ffn2=g_ffn2, w_ffn2_gate=w_ffn2_gate, w_ffn2_up=w_ffn2_up, w_ffn2_down=w_ffn2_down,
             g_pe=g_pe, w_pe_gate=w_pe_gate, w_pe_proj=w_pe_proj, w_in0=w_in0, ssd_conv_w=ssd_conv_w,
             ssd_conv_b=ssd_conv_b, ssd_a_log=ssd_a_log, ssd_dt_bias=ssd_dt_bias, ssd_d=ssd_d,
             ssd_norm_g=ssd_norm_g, s5_lam_re=s5_lam_re, s5_lam_im=s5_lam_im, s5_log_step=s5_log_step,
             s5_b_re=s5_b_re, s5_b_im=s5_b_im, s5_c_re=s5_c_re, s5_c_im=s5_c_im, s5_d=s5_d,
             s5_w_glu=s5_w_glu, s5_b_glu=s5_b_glu, w_out0=w_out0, w_in1=w_in1, sc_conv_w=sc_conv_w,
             w_out1=w_out1, g_final=g_final, ssd_shape=state_ssd.shape[2:])
    wb = _prepare_weights(w)
    b = x_prompt.shape[0]
    zeros = lambda ref: jnp.zeros((ref.shape[0], b) + ref.shape[2:], F32)
    prompt = _run_trunk(x_prompt, p_prompt, zeros(cache_conv_ssd), zeros(state_ssd), zeros(state_s5_re),
                        zeros(state_s5_im), zeros(cache_conv_short), w, wb)
    sample = _run_trunk(x_sample, p_sample, cache_conv_ssd, state_ssd, state_s5_re, state_s5_im, cache_conv_short, w, wb)
    return (prompt[0], sample[0]) + prompt[1:] + sample[1:]
```

```python
import functools

import jax
import jax.numpy as jnp
from jax import lax
from jax.experimental import pallas as pl
from jax.experimental.pallas import tpu as pltpu

F32 = jnp.float32
BF16 = jnp.bfloat16
EPS = 1e-6
SSD_CHUNK = 64
S5_GROUP_BLOCK = 16
S5_SCAN_UNROLL = 8
LANES = 128
SUBLANES = 8
V7X_VMEM_BYTES = 64 * 1024 * 1024
MM_VMEM_BUDGET = 40 * 1024 * 1024
MXU_WIDTH = 256
MXU_FLOPS = 0.95e15
HBM_BYTES_PER_S = 3.0e12
GRID_STEP_S = 0.35e-6
HIGHEST = lax.Precision.HIGHEST
NT_DIMS = (((1,), (1,)), ((), ()))
TN_DIMS = (((0,), (0,)), ((), ()))


def _largest_divisor(n, candidates):
    for c in candidates:
        if n % c == 0:
            return c
    raise ValueError(f"no tile in {candidates} divides {n}")


def _round_up(n, m):
    return (n + m - 1) // m * m


def _params(semantics, vmem_estimate):
    limit = min(V7X_VMEM_BYTES - 8 * 1024 * 1024, int(vmem_estimate * 1.25) + 8 * 1024 * 1024)
    return pltpu.CompilerParams(dimension_semantics=semantics, vmem_limit_bytes=limit)


def _silu(x):
    return x * jax.nn.sigmoid(x)


def _rmsnorm_body(x_ref, g_ref, o_ref):
    x = x_ref[...]
    inv = lax.rsqrt(jnp.mean(x * x, axis=-1, keepdims=True) + EPS)
    o_ref[...] = (x * inv * g_ref[...]).astype(o_ref.dtype)


def _rmsnorm(x, g, out_dtype):
    m, d = x.shape
    tm = _largest_divisor(m, (256, 128, 64, 32, 16, 8))
    return pl.pallas_call(
        _rmsnorm_body,
        out_shape=jax.ShapeDtypeStruct((m, d), out_dtype),
        grid=(m // tm,),
        in_specs=[pl.BlockSpec((tm, d), lambda i: (i, 0)), pl.BlockSpec((1, d), lambda i: (0, 0))],
        out_specs=pl.BlockSpec((tm, d), lambda i: (i, 0)),
        compiler_params=_params(("parallel",), 2 * tm * d * (4 + jnp.dtype(out_dtype).itemsize) + 3 * tm * d * 4),
        name="rmsnorm",
    )(x, g.reshape(1, d))


def _mm_body(*refs, n_a, n_w, dots, epilogue):
    a_refs, w_refs, e_refs, o_ref = refs[:n_a], refs[n_a:n_a + n_w], refs[n_a + n_w:-1], refs[-1]
    a_vals = {}
    accs = []
    for ai, wi in dots:
        if ai not in a_vals:
            a_vals[ai] = a_refs[ai][...].astype(BF16)
        accs.append(jnp.dot(a_vals[ai], w_refs[wi][...], preferred_element_type=F32))
    o_ref[...] = epilogue(accs, [e[...] for e in e_refs]).astype(o_ref.dtype)


def _mm_vmem(tm, tn, a_list, w_list, n_dots, tiles, out_dtype):
    a_bytes = sum(tm * a.shape[1] * a.dtype.itemsize for a in a_list)
    w_bytes = sum(w.shape[0] * tn * w.dtype.itemsize for w in w_list)
    io_bytes = tm * tn * (jnp.dtype(out_dtype).itemsize + 4 * tiles)
    return 2 * (a_bytes + w_bytes + io_bytes) + (n_dots + 1) * tm * tn * 4


def _fused_matmul(a_list, w_list, dots, extras, epilogue, out_dtype, name):
    m, n = a_list[0].shape[0], w_list[0].shape[1]
    n_tiles = sum(1 for e in extras if e.shape[0] != 1)
    flops = 2.0 * m * n * sum(w_list[wi].shape[0] for _, wi in dots)
    a_bytes = sum(a.size * a.dtype.itemsize for a in a_list)
    w_bytes = sum(w.size * w.dtype.itemsize for w in w_list)
    io_bytes = m * n * (jnp.dtype(out_dtype).itemsize + 4 * n_tiles)
    choice = None
    for tm in (1024, 512, 256, 128, 64, 32, 16, 8):
        for tn in (1024, 512, 256, 128):
            if m % tm or n % tn:
                continue
            est = _mm_vmem(tm, tn, a_list, w_list, len(dots), n_tiles, out_dtype)
            if est > MM_VMEM_BUDGET:
                continue
            t_mxu = flops / MXU_FLOPS * (1.0 if tn % MXU_WIDTH == 0 and tm % MXU_WIDTH == 0 else 2.0)
            t_hbm = (a_bytes + (m // tm) * w_bytes + io_bytes) / HBM_BYTES_PER_S
            cost = max(t_mxu, t_hbm) + (m // tm) * (n // tn) * GRID_STEP_S
            if choice is None or cost < choice[0]:
                choice = (cost, tm, tn, est)
    if choice is None:
        raise ValueError(f"{name}: no matmul tiling fits VMEM")
    _, tm, tn, est = choice
    in_specs = [pl.BlockSpec((tm, a.shape[1]), lambda i, j: (i, 0)) for a in a_list]
    in_specs += [pl.BlockSpec((w.shape[0], tn), lambda i, j: (0, j)) for w in w_list]
    for e in extras:
        if e.shape[0] == 1:
            in_specs.append(pl.BlockSpec((1, tn), lambda i, j: (0, j)))
        else:
            in_specs.append(pl.BlockSpec((tm, tn), lambda i, j: (i, j)))
    body = functools.partial(_mm_body, n_a=len(a_list), n_w=len(w_list), dots=tuple(dots), epilogue=epilogue)
    return pl.pallas_call(
        body,
        out_shape=jax.ShapeDtypeStruct((m, n), out_dtype),
        grid=(m // tm, n // tn),
        in_specs=in_specs,
        out_specs=pl.BlockSpec((tm, tn), lambda i, j: (i, j)),
        compiler_params=_params(("parallel", "parallel"), est),
        name=name,
    )(*a_list, *w_list, *extras)


def _ep_plain(accs, ex):
    return accs[0]


def _ep_swiglu(accs, ex):
    return _silu(accs[0]) * accs[1]


def _ep_half_residual(accs, ex):
    return ex[0] + 0.5 * accs[0]


def _ep_residual(accs, ex):
    return ex[0] + accs[0]


def _ep_residual2(accs, ex):
    return ex[0] + (accs[0] + accs[1])


def _ep_glu(accs, ex):
    return ex[0] * jax.nn.sigmoid(accs[0] + ex[1])


def _ep_gated_embed(accs, ex):
    return ex[0] + jax.nn.sigmoid(accs[0]) * accs[1]


def _ssd_body(z_ref, xs_ref, bm_ref, cm_ref, dtc_ref, dtr_ref, hx_ref, hb_ref, hc_ref, h0_ref,
              wx_ref, wb_ref, wc_ref, bx_ref, bb_ref, bc_ref, alc_ref, alr_ref, dbc_ref, dbr_ref, drep_ref, ng_ref,
              ya_ref, hfin_ref, state_sc, xp_sc, y_sc, *, heads, head_dim, n_state, chunk, conv_k):
    c = pl.program_id(2)
    q, p, n = chunk, head_dim, n_state
    w = heads * p
    t0 = SUBLANES - (conv_k - 1)

    @pl.when(c == 0)
    def _load_carried():
        state_sc[...] = h0_ref[...]
        xp_sc[t0:SUBLANES, 0:w] = hx_ref[...]
        xp_sc[t0:SUBLANES, w:w + n] = hb_ref[...]
        xp_sc[t0:SUBLANES, w + n:w + 2 * n] = hc_ref[...]

    xp_sc[SUBLANES:SUBLANES + q, 0:w] = xs_ref[...]
    xp_sc[SUBLANES:SUBLANES + q, w:w + n] = bm_ref[...]
    xp_sc[SUBLANES:SUBLANES + q, w + n:w + 2 * n] = cm_ref[...]

    def conv_silu(lo, hi, w_ref, b_ref):
        acc = xp_sc[t0:t0 + q, lo:hi] * w_ref[0:1, :]
        for j in range(1, conv_k):
            acc = acc + xp_sc[t0 + j:t0 + j + q, lo:hi] * w_ref[j:j + 1, :]
        return _silu(acc + b_ref[...])

    xs = conv_silu(0, w, wx_ref, bx_ref)
    bm = conv_silu(w, w + n, wb_ref, bb_ref)
    cm = conv_silu(w + n, w + 2 * n, wc_ref, bc_ref)
    xp_sc[t0:SUBLANES, :] = xp_sc[SUBLANES + q - (conv_k - 1):SUBLANES + q, :]

    dt_c = jax.nn.softplus(dtc_ref[...] + dbc_ref[...])
    dt_r = jax.nn.softplus(dtr_ref[...] + dbr_ref[...])
    a_c = -jnp.exp(alc_ref[...])
    a_r = -jnp.exp(alr_ref[...])
    ii = lax.broadcasted_iota(jnp.int32, (q, q), 0)
    jj = lax.broadcasted_iota(jnp.int32, (q, q), 1)
    causal = ii >= jj
    acum_c = jnp.dot(causal.astype(F32), dt_c * a_c, precision=HIGHEST, preferred_element_type=F32)
    acum_r = jnp.dot(dt_r * a_r, (ii <= jj).astype(F32), precision=HIGHEST, preferred_element_type=F32)
    alast_c = acum_c[q - 1:q, :]
    grow_c = jnp.exp(acum_c)
    wend_c = jnp.exp(alast_c - acum_c) * dt_c
    elast_c = jnp.exp(alast_c)

    cm_b = cm.astype(BF16)
    bm_b = bm.astype(BF16)
    scores = lax.dot_general(cm_b, bm_b, NT_DIMS, preferred_element_type=F32)
    for h in range(heads):
        seg = acum_c[:, h:h + 1] - acum_r[h:h + 1, :]
        decay = jnp.exp(jnp.where(causal, seg, -jnp.inf))
        m_h = (scores * decay * dt_r[h:h + 1, :]).astype(BF16)
        x_h = xs[:, h * p:(h + 1) * p]
        st = state_sc[h]
        y_off = lax.dot_general(cm_b, st.astype(BF16), NT_DIMS, preferred_element_type=F32)
        y_sc[:, h * p:(h + 1) * p] = (jnp.dot(m_h, x_h.astype(BF16), preferred_element_type=F32)
                                      + y_off * grow_c[:, h:h + 1])
        xw = (x_h * wend_c[:, h:h + 1]).astype(BF16)
        state_sc[h] = st * elast_c[:, h:h + 1] + lax.dot_general(xw, bm_b, TN_DIMS, preferred_element_type=F32)

    y = y_sc[...] + drep_ref[...] * xs
    y = y * _silu(z_ref[...])
    inv = lax.rsqrt(jnp.mean(y * y, axis=-1, keepdims=True) + EPS)
    ya_ref[...] = (y * inv * ng_ref[...]).astype(ya_ref.dtype)

    @pl.when(c == pl.num_programs(2) - 1)
    def _store_state():
        hfin_ref[...] = state_sc[...]


def _ssd_mixer(zx, dt_raw, conv_hist, h0, conv_w, conv_b, a_log, dt_bias, d_skip, norm_g):
    b, l, _ = zx.shape
    _, heads, p, n = h0.shape
    inner = heads * p
    conv_k, conv_dim = conv_w.shape
    g = (conv_dim - inner) // (2 * n)
    r = heads // g
    w = r * p
    q = SSD_CHUNK if l % SSD_CHUNK == 0 else l
    nc = l // q
    dt5 = dt_raw.reshape(b, nc, q, g, r)
    dtc = dt5.transpose(0, 3, 1, 2, 4)
    dtr = dt5.transpose(0, 3, 1, 4, 2)
    conv_b2 = conv_b.reshape(1, conv_dim)
    nb0 = inner // n

    def col(block_w, first):
        return pl.BlockSpec((None, q, block_w), lambda bi, gi, ci: (bi, ci, first + gi))

    def hist(block_w, first):
        return pl.BlockSpec((None, conv_k - 1, block_w), lambda bi, gi, ci: (bi, 0, first + gi))

    def wrow(rows, block_w, first):
        return pl.BlockSpec((rows, block_w), lambda bi, gi, ci: (0, first + gi))

    def per_group(shape):
        return pl.BlockSpec((None,) + shape, lambda bi, gi, ci: (gi, 0, 0))

    in_specs = [
        col(w, 0), col(w, inner // w), col(n, 2 * inner // n), col(n, 2 * inner // n + g),
        pl.BlockSpec((None, None, None, q, r), lambda bi, gi, ci: (bi, gi, ci, 0, 0)),
        pl.BlockSpec((None, None, None, r, q), lambda bi, gi, ci: (bi, gi, ci, 0, 0)),
        hist(w, 0), hist(n, nb0), hist(n, nb0 + g),
        pl.BlockSpec((None, r, p, n), lambda bi, gi, ci: (bi, gi, 0, 0)),
        wrow(conv_k, w, 0), wrow(conv_k, n, nb0), wrow(conv_k, n, nb0 + g),
        wrow(1, w, 0), wrow(1, n, nb0), wrow(1, n, nb0 + g),
        per_group((1, r)), per_group((r, 1)), per_group((1, r)), per_group((r, 1)),
        per_group((1, w)), per_group((1, w)),
    ]
    out_specs = [
        pl.BlockSpec((None, q, w), lambda bi, gi, ci: (bi, ci, gi)),
        pl.BlockSpec((None, r, p, n), lambda bi, gi, ci: (bi, gi, 0, 0)),
    ]
    est = 4 * (2 * q * (2 * w + 2 * n) + 6 * r * p * n + (SUBLANES + q) * (w + 2 * n) + 8 * q * w)
    body = functools.partial(_ssd_body, heads=r, head_dim=p, n_state=n, chunk=q, conv_k=conv_k)
    ya, h_fin = pl.pallas_call(
        body,
        out_shape=[jax.ShapeDtypeStruct((b, l, inner), BF16), jax.ShapeDtypeStruct(h0.shape, F32)],
        grid=(b, g, nc),
        in_specs=in_specs,
        out_specs=out_specs,
        scratch_shapes=[pltpu.VMEM((r, p, n), F32), pltpu.VMEM((SUBLANES + q, w + 2 * n), F32),
                        pltpu.VMEM((q, w), F32)],
        compiler_params=_params(("parallel", "parallel", "arbitrary"), est),
        name="ssd_mixer",
    )(zx, zx, zx, zx, dtc, dtr, conv_hist, conv_hist, conv_hist, h0,
      conv_w, conv_w, conv_w, conv_b2, conv_b2, conv_b2,
      a_log.reshape(g, 1, r), a_log.reshape(g, r, 1), dt_bias.reshape(g, 1, r), dt_bias.reshape(g, r, 1),
      jnp.repeat(d_skip, p).reshape(g, 1, w), norm_g.reshape(g, 1, w))
    return ya, h_fin


def _gelu_tanh(x):
    return 0.5 * x * (1.0 + jnp.tanh(0.7978845608028654 * (x + 0.044715 * (x * x * x))))


def _s5_body(u_ref, h0re_ref, h0im_ref, lam_ref, bblk_ref, cblk_ref, d_ref,
             y_ref, ybf_ref, hre_ref, him_ref, bu_sc, h_sc, *, n_seq, n_pack, t_blk, pitch, n_tiles, unroll):
    j = pl.program_id(1)
    half = n_tiles // 2
    rows = n_seq * t_blk
    cw = u_ref.shape[-1] // n_pack
    sw = half * LANES

    @pl.when(j == 0)
    def _load_state():
        for pk in range(n_pack):
            for k in range(half):
                cols = slice(pk * sw + k * LANES, pk * sw + (k + 1) * LANES)
                h_sc[k, pk * n_seq:(pk + 1) * n_seq, :] = h0re_ref[:, cols]
                h_sc[half + k, pk * n_seq:(pk + 1) * n_seq, :] = h0im_ref[:, cols]

    for pk in range(n_pack):
        u_b = u_ref[:, :, pk * cw:(pk + 1) * cw].reshape(rows, cw).astype(BF16)
        for kk in range(half):
            bu = jnp.dot(u_b, bblk_ref[pk, :, kk * 2 * LANES:(kk + 1) * 2 * LANES], preferred_element_type=F32)
            for s in range(n_seq):
                base = (pk * n_seq + s) * pitch
                bu_sc[2 * kk, base:base + t_blk, :] = bu[s * t_blk:(s + 1) * t_blk, :LANES]
                bu_sc[2 * kk + 1, base:base + t_blk, :] = bu[s * t_blk:(s + 1) * t_blk, LANES:]

    def lam_rows(k):
        return jnp.concatenate([jnp.broadcast_to(lam_ref[pk, :, k * LANES:(k + 1) * LANES], (n_seq, LANES))
                                for pk in range(n_pack)], axis=0)

    lam_re = [lam_rows(k) for k in range(half)]
    lam_im = [lam_rows(half + k) for k in range(half)]

    def step(t, hs):
        idx = pl.ds(t, n_pack * n_seq, stride=pitch)
        new_re, new_im = [], []
        for k in range(half):
            h_re, h_im = hs[k], hs[half + k]
            re_rows, im_rows = bu_sc.at[k], bu_sc.at[half + k]
            n_re = lam_re[k] * h_re - lam_im[k] * h_im + re_rows[idx, :]
            n_im = lam_re[k] * h_im + lam_im[k] * h_re + im_rows[idx, :]
            re_rows[idx, :] = n_re
            im_rows[idx, :] = n_im
            new_re.append(n_re)
            new_im.append(n_im)
        return tuple(new_re + new_im)

    hs = lax.fori_loop(0, t_blk, step, tuple(h_sc[k] for k in range(n_tiles)), unroll=unroll)
    for k in range(n_tiles):
        h_sc[k] = hs[k]

    for pk in range(n_pack):
        h_all = jnp.concatenate(
            [jnp.concatenate([bu_sc[k, (pk * n_seq + s) * pitch:(pk * n_seq + s) * pitch + t_blk, :].astype(BF16)
                              for s in range(n_seq)], axis=0) for k in range(n_tiles)], axis=1)
        u = u_ref[:, :, pk * cw:(pk + 1) * cw].reshape(rows, cw)
        y = jnp.dot(h_all, cblk_ref[pk], preferred_element_type=F32) + d_ref[:, pk * cw:(pk + 1) * cw] * u
        y = _gelu_tanh(y)
        y_ref[:, :, pk * cw:(pk + 1) * cw] = y.reshape(n_seq, t_blk, cw)
        ybf_ref[:, :, pk * cw:(pk + 1) * cw] = y.astype(BF16).reshape(n_seq, t_blk, cw)

    @pl.when(j == pl.num_programs(1) - 1)
    def _store_state():
        for pk in range(n_pack):
            for k in range(half):
                cols = slice(pk * sw + k * LANES, pk * sw + (k + 1) * LANES)
                hre_ref[:, cols] = hs[k][pk * n_seq:(pk + 1) * n_seq, :]
                him_ref[:, cols] = hs[half + k][pk * n_seq:(pk + 1) * n_seq, :]


def _s5_discretize(lam_re, lam_im, log_step, b_re, b_im, c_re, c_im):
    g, p, k = b_re.shape
    gb = S5_GROUP_BLOCK
    nb = g // gb
    step = jnp.exp(log_step)[:, None]
    mag = jnp.exp(lam_re * step)
    lb_re, lb_im = mag * jnp.cos(lam_im * step), mag * jnp.sin(lam_im * step)
    den = lam_re * lam_re + lam_im * lam_im
    f_re = ((lb_re - 1.0) * lam_re + lb_im * lam_im) / den
    f_im = (lb_im * lam_re - (lb_re - 1.0) * lam_im) / den
    bb_re = f_re[..., None] * b_re - f_im[..., None] * b_im
    bb_im = f_re[..., None] * b_im + f_im[..., None] * b_re
    eye = jnp.eye(gb, dtype=F32)
    lam = jnp.stack([lb_re.reshape(nb, gb * p), lb_im.reshape(nb, gb * p)], axis=1).reshape(nb, 1, 2 * gb * p)
    bb = jnp.stack([bb_re, bb_im]).reshape(2, nb, gb, p, k).transpose(1, 2, 4, 0, 3)
    bblk = (bb[:, :, :, :, None, :] * eye[None, :, None, None, :, None]).reshape(nb, gb * k, 2 * gb * p)
    cc = jnp.stack([c_re, -c_im]).reshape(2, nb, gb, k, p).transpose(1, 0, 2, 4, 3)
    cblk = (cc[:, :, :, :, None, :] * eye[None, None, :, None, :, None]).reshape(nb, 2 * gb * p, gb * k)
    return lam, bblk.astype(BF16), cblk.astype(BF16)


def _s5_mixer(u, h0_re, h0_im, lam, bblk, cblk, d_skip):
    b, l, width = u.shape
    _, g, p = h0_re.shape
    nb, cw, sw2 = bblk.shape
    n_tiles = sw2 // LANES
    n_pack = max(1, SUBLANES // b)
    if nb % n_pack:
        n_pack = 1
    t_blk = _largest_divisor(l, (256, 128, 64, 32, 16, 8))
    rows = n_pack * b * t_blk
    pitch = SUBLANES * (t_blk // SUBLANES + (1 - (t_blk // SUBLANES) % 2))
    est = 6 * rows * sw2 + 2 * rows * cw * 10 + 8 * n_pack * cw * sw2 + 6 * rows * cw * 4
    body = functools.partial(_s5_body, n_seq=b, n_pack=n_pack, t_blk=t_blk, pitch=pitch, n_tiles=n_tiles,
                             unroll=S5_SCAN_UNROLL)
    seq_spec = pl.BlockSpec((b, t_blk, n_pack * cw), lambda gi, ti: (0, ti, gi))
    state_spec = pl.BlockSpec((b, n_pack * sw2 // 2), lambda gi, ti: (0, gi))
    y, ybf, h_re, h_im = pl.pallas_call(
        body,
        out_shape=[jax.ShapeDtypeStruct((b, l, width), F32), jax.ShapeDtypeStruct((b, l, width), BF16),
                   jax.ShapeDtypeStruct((b, g * p), F32), jax.ShapeDtypeStruct((b, g * p), F32)],
        grid=(nb // n_pack, l // t_blk),
        in_specs=[seq_spec, state_spec, state_spec,
                  pl.BlockSpec((n_pack, 1, sw2), lambda gi, ti: (gi, 0, 0)),
                  pl.BlockSpec((n_pack, cw, sw2), lambda gi, ti: (gi, 0, 0)),
                  pl.BlockSpec((n_pack, sw2, cw), lambda gi, ti: (gi, 0, 0)),
                  pl.BlockSpec((1, n_pack * cw), lambda gi, ti: (0, gi))],
        out_specs=[seq_spec, seq_spec, state_spec, state_spec],
        scratch_shapes=[pltpu.VMEM((n_tiles, n_pack * b * pitch, LANES), F32),
                        pltpu.VMEM((n_tiles, n_pack * b, LANES), F32)],
        compiler_params=_params(("parallel", "arbitrary"), est),
        name="s5_mixer",
    )(u, h0_re.reshape(b, g * p), h0_im.reshape(b, g * p), lam, bblk, cblk, d_skip.reshape(1, width))
    return y, ybf, h_re.reshape(b, g, p), h_im.reshape(b, g, p)


def _short_conv_body(b_ref, c_ref, v_ref, hist_ref, w_ref, o_ref, hnew_ref, cv_sc, *, t_blk, conv_k):
    t = pl.program_id(2)
    t0 = SUBLANES - (conv_k - 1)

    @pl.when(t == 0)
    def _load_hist():
        cv_sc[t0:SUBLANES, :] = hist_ref[...]

    cv_sc[SUBLANES:SUBLANES + t_blk, :] = c_ref[...] * v_ref[...]
    acc = cv_sc[t0:t0 + t_blk, :] * w_ref[0:1, :]
    for j in range(1, conv_k):
        acc = acc + cv_sc[t0 + j:t0 + j + t_blk, :] * w_ref[j:j + 1, :]
    o_ref[...] = (b_ref[...] * acc).astype(o_ref.dtype)
    cv_sc[t0:SUBLANES, :] = cv_sc[SUBLANES + t_blk - (conv_k - 1):SUBLANES + t_blk, :]

    @pl.when(t == pl.num_programs(2) - 1)
    def _store_hist():
        hnew_ref[...] = cv_sc[t0:SUBLANES, :]


def _short_conv(bcv, hist, conv_w):
    bsz, l, width3 = bcv.shape
    width = width3 // 3
    conv_k = conv_w.shape[0]
    t_blk = _largest_divisor(l, (256, 128, 64, 32, 16, 8))
    tw = _largest_divisor(width, (1024, 512, 256, 128))
    nw = width // tw

    def col(first):
        return pl.BlockSpec((None, t_blk, tw), lambda bi, wi, ti: (bi, ti, first + wi))

    body = functools.partial(_short_conv_body, t_blk=t_blk, conv_k=conv_k)
    return pl.pallas_call(
        body,
        out_shape=[jax.ShapeDtypeStruct((bsz, l, width), BF16), jax.ShapeDtypeStruct(hist.shape, F32)],
        grid=(bsz, nw, l // t_blk),
        in_specs=[col(0), col(nw), col(2 * nw),
                  pl.BlockSpec((None, conv_k - 1, tw), lambda bi, wi, ti: (bi, 0, wi)),
                  pl.BlockSpec((conv_k, tw), lambda bi, wi, ti: (0, wi))],
        out_specs=[pl.BlockSpec((None, t_blk, tw), lambda bi, wi, ti: (bi, ti, wi)),
                   pl.BlockSpec((None, conv_k - 1, tw), lambda bi, wi, ti: (bi, 0, wi))],
        scratch_shapes=[pltpu.VMEM((SUBLANES + t_blk, tw), F32)],
        compiler_params=_params(("parallel", "parallel", "arbitrary"), 4 * t_blk * tw * 12),
        name="short_conv",
    )(bcv, bcv, bcv, hist, conv_w)


def _prepare_weights(w):
    depth, d_model, d_ff = w["w_ffn1_gate"].shape
    ffp = _round_up(d_ff, 1024)
    heads, p, _ = w["ssd_shape"]
    inner = heads * p
    conv_dim = w["ssd_conv_w"].shape[-1]
    out = {}
    for name in ("w_ffn1_gate", "w_ffn1_up", "w_ffn2_gate", "w_ffn2_up"):
        out[name] = jnp.pad(w[name].astype(BF16), ((0, 0), (0, 0), (0, ffp - d_ff)))
    for name in ("w_ffn1_down", "w_ffn2_down"):
        out[name] = jnp.pad(w[name].astype(BF16), ((0, 0), (0, ffp - d_ff), (0, 0)))
    for name in ("w_pe_gate", "w_pe_proj", "s5_w_glu", "w_in1", "w_out1"):
        out[name] = w[name].astype(BF16)
    w_in0 = w["w_in0"]
    s1 = inner + conv_dim
    out["w_zx"] = w_in0[:, :, :s1].astype(BF16)
    out["w_dt"] = jnp.pad(w_in0[:, :, s1:s1 + heads].astype(BF16), ((0, 0), (0, 0), (0, _round_up(heads, LANES) - heads)))
    out["w_u"] = w_in0[:, :, s1 + heads:].astype(BF16)
    out["w_out0_a"] = w["w_out0"][:, :inner].astype(BF16)
    out["w_out0_b"] = w["w_out0"][:, inner:].astype(BF16)
    out["s5"] = [_s5_discretize(w["s5_lam_re"][j], w["s5_lam_im"][j], w["s5_log_step"][j], w["s5_b_re"][j],
                                w["s5_b_im"][j], w["s5_c_re"][j], w["s5_c_im"][j])
                 for j in range(w["s5_lam_re"].shape[0])]
    return out


def _ffn_half_step(x, g, w_gate, w_up, w_down, tag):
    xn = _rmsnorm(x, g, BF16)
    h = _fused_matmul([xn], [w_gate, w_up], [(0, 0), (0, 1)], [], _ep_swiglu, BF16, f"{tag}_gate_up")
    return _fused_matmul([h], [w_down], [(0, 0)], [x], _ep_half_residual, F32, f"{tag}_down")


def _run_trunk(x3, p4, conv_ssd, h_ssd, s5_re, s5_im, conv_sc, w, wb):
    bsz, l, d = x3.shape
    m = bsz * l
    depth = w["g_ffn1"].shape[0]
    heads = h_ssd.shape[2]
    x = x3.reshape(m, d)
    out_conv_ssd, out_h, out_re, out_im, out_sc = [], [], [], [], []
    for i in range(depth):
        x = _ffn_half_step(x, w["g_ffn1"][i], wb["w_ffn1_gate"][i], wb["w_ffn1_up"][i], wb["w_ffn1_down"][i], "ffn1")
        xn = _rmsnorm(x, w["g_mix"][i], BF16)
        j = i // 2
        if i % 2 == 0:
            zx = _fused_matmul([xn], [wb["w_zx"][j]], [(0, 0)], [], _ep_plain, F32, "in0_zxbc")
            dt = _fused_matmul([xn], [wb["w_dt"][j]], [(0, 0)], [], _ep_plain, F32, "in0_dt")[:, :heads]
            u = _fused_matmul([xn], [wb["w_u"][j]], [(0, 0)], [], _ep_plain, F32, "in0_u")
            zx3 = zx.reshape(bsz, l, zx.shape[-1])
            ya, h_new = _ssd_mixer(zx3, dt.reshape(bsz, l, heads), conv_ssd[j], h_ssd[j], w["ssd_conv_w"][j],
                                   w["ssd_conv_b"][j], w["ssd_a_log"][j], w["ssd_dt_bias"][j], w["ssd_d"][j],
                                   w["ssd_norm_g"][j])
            inner = ya.shape[-1]
            conv_k, conv_dim = w["ssd_conv_w"][j].shape
            xbc_tail = zx3[:, l - min(l, conv_k - 1):, inner:inner + conv_dim]
            c_new = jnp.concatenate([conv_ssd[j], xbc_tail], axis=1)[:, -(conv_k - 1):]
            lam, bblk, cblk = wb["s5"][j]
            yb, yb_bf, re_new, im_new = _s5_mixer(u.reshape(bsz, l, u.shape[-1]), s5_re[j], s5_im[j], lam, bblk, cblk,
                                                  w["s5_d"][j])
            width = yb.shape[-1]
            yb_glu = _fused_matmul([yb_bf.reshape(m, width)], [wb["s5_w_glu"][j]], [(0, 0)],
                                   [yb.reshape(m, width), w["s5_b_glu"][j].reshape(1, width)], _ep_glu, BF16, "s5_glu")
            x = _fused_matmul([ya.reshape(m, inner), yb_glu], [wb["w_out0_a"][j], wb["w_out0_b"][j]],
                              [(0, 0), (1, 1)], [x], _ep_residual2, F32, "out0")
            out_conv_ssd.append(c_new)
            out_h.append(h_new)
            out_re.append(re_new)
            out_im.append(im_new)
        else:
            bcv = _fused_matmul([xn], [wb["w_in1"][j]], [(0, 0)], [], _ep_plain, F32, "in1")
            yc, sc_new = _short_conv(bcv.reshape(bsz, l, bcv.shape[-1]), conv_sc[j], w["sc_conv_w"][j])
            x = _fused_matmul([yc.reshape(m, yc.shape[-1])], [wb["w_out1"][j]], [(0, 0)], [x], _ep_residual, F32, "out1")
            out_sc.append(sc_new)
        x = _ffn_half_step(x, w["g_ffn2"][i], wb["w_ffn2_gate"][i], wb["w_ffn2_up"][i], wb["w_ffn2_down"][i], "ffn2")
        xn = _rmsnorm(x, w["g_pe"][i], BF16)
        x = _fused_matmul([xn, p4[i].reshape(m, p4.shape[-1])], [wb["w_pe_gate"][i], wb["w_pe_proj"][i]],
                          [(0, 0), (1, 1)], [x], _ep_gated_embed, F32, "pe_embed")
    y = _rmsnorm(x, w["g_final"], F32).reshape(bsz, l, d)
    return (y, jnp.stack(out_conv_ssd), jnp.stack(out_h), jnp.stack(out_re), jnp.stack(out_im), jnp.stack(out_sc))


def kernel(x_prompt, x_sample, p_prompt, p_sample, cache_conv_ssd, state_ssd, state_s5_re, state_s5_im, cache_conv_short, g_ffn1, w_ffn1_gate, w_ffn1_up, w_ffn1_down, g_mix, g_ffn2, w_ffn2_gate, w_ffn2_up, w_ffn2_down, g_pe, w_pe_gate, w_pe_proj, w_in0, ssd_conv_w, ssd_conv_b, ssd_a_log, ssd_dt_bias, ssd_d, ssd_norm_g, s5_lam_re, s5_lam_im, s5_log_step, s5_b_re, s5_b_im, s5_c_re, s5_c_im, s5_d, s5_w_glu, s5_b_glu, w_out0, w_in1, sc_conv_w, w_out1, g_final):
    w = dict(g_ffn1=g_ffn1, w_ffn1_gate=w_ffn1_gate, w_ffn1_up=w_ffn1_up, w_ffn1_down=w_ffn1_down,
             g_mix=g_mix, g_ffn2=g_ffn2, w_ffn2_gate=w_ffn2_gate, w_ffn2_up=w_ffn2_up, w_ffn2_down=w_ffn2_down,
             g_pe=g_pe, w_pe_gate=w_pe_gate, w_pe_proj=w_pe_proj, w_in0=w_in0, ssd_conv_w=ssd_conv_w,
             ssd_conv_b=ssd_conv_b, ssd_a_log=ssd_a_log, ssd_dt_bias=ssd_dt_bias, ssd_d=ssd_d,
             ssd_norm_g=ssd_norm_g, s5_lam_re=s5_lam_re, s5_lam_im=s5_lam_im, s5_log_step=s5_log_step,
             s5_b_re=s5_b_re, s5_b_im=s5_b_im, s5_c_re=s5_c_re, s5_c_im=s5_c_im, s5_d=s5_d,
             s5_w_glu=s5_w_glu, s5_b_glu=s5_b_glu, w_out0=w_out0, w_in1=w_in1, sc_conv_w=sc_conv_w,
             w_out1=w_out1, g_final=g_final, ssd_shape=state_ssd.shape[2:])
    wb = _prepare_weights(w)
    b = x_prompt.shape[0]
    zeros = lambda ref: jnp.zeros((ref.shape[0], b) + ref.shape[2:], F32)
    prompt = _run_trunk(x_prompt, p_prompt, zeros(cache_conv_ssd), zeros(state_ssd), zeros(state_s5_re),
                        zeros(state_s5_im), zeros(cache_conv_short), w, wb)
    sample = _run_trunk(x_sample, p_sample, cache_conv_ssd, state_ssd, state_s5_re, state_s5_im, cache_conv_short, w, wb)
    return (prompt[0], sample[0]) + prompt[1:] + sample[1:]
```

```python
import functools

import jax
import jax.numpy as jnp
from jax import lax
from jax.experimental import pallas as pl
from jax.experimental.pallas import tpu as pltpu

F32 = jnp.float32
BF16 = jnp.bfloat16
EPS = 1e-6
SSD_CHUNK = 64
S5_GROUP_BLOCK = 16
S5_SCAN_UNROLL = 8
LANES = 128
SUBLANES = 8
V7X_VMEM_BYTES = 64 * 1024 * 1024
MM_VMEM_BUDGET = 42 * 1024 * 1024
MXU_WIDTH = 256
MXU_FLOPS = 0.95e15
HBM_BYTES_PER_S = 3.0e12
GRID_STEP_S = 0.35e-6
HIGHEST = lax.Precision.HIGHEST
NT_DIMS = (((1,), (1,)), ((), ()))
TN_DIMS = (((0,), (0,)), ((), ()))


def _largest_divisor(n, candidates):
    for c in candidates:
        if n % c == 0:
            return c
    raise ValueError(f"no tile in {candidates} divides {n}")


def _round_up(n, m):
    return (n + m - 1) // m * m


def _params(semantics, vmem_estimate):
    limit = min(V7X_VMEM_BYTES - 8 * 1024 * 1024, int(vmem_estimate * 1.25) + 8 * 1024 * 1024)
    return pltpu.CompilerParams(dimension_semantics=semantics, vmem_limit_bytes=limit)


def _silu(x):
    return x * jax.nn.sigmoid(x)


def _rmsnorm_body(x_ref, g_ref, o_ref):
    x = x_ref[...]
    inv = lax.rsqrt(jnp.mean(x * x, axis=-1, keepdims=True) + EPS)
    o_ref[...] = (x * inv * g_ref[...]).astype(o_ref.dtype)


def _rmsnorm(x, g, out_dtype):
    m, d = x.shape
    tm = _largest_divisor(m, (256, 128, 64, 32, 16, 8))
    return pl.pallas_call(
        _rmsnorm_body,
        out_shape=jax.ShapeDtypeStruct((m, d), out_dtype),
        grid=(m // tm,),
        in_specs=[pl.BlockSpec((tm, d), lambda i: (i, 0)), pl.BlockSpec((1, d), lambda i: (0, 0))],
        out_specs=pl.BlockSpec((tm, d), lambda i: (i, 0)),
        compiler_params=_params(("parallel",), 2 * tm * d * (4 + jnp.dtype(out_dtype).itemsize) + 3 * tm * d * 4),
        name="rmsnorm",
    )(x, g.reshape(1, d))


def _mm_body(*refs, n_a, n_w, n_e, dots, epilogue):
    a_refs, w_refs, e_refs = refs[:n_a], refs[n_a:n_a + n_w], refs[n_a + n_w:n_a + n_w + n_e]
    o_ref, w_out_refs = refs[n_a + n_w + n_e], refs[n_a + n_w + n_e + 1:]
    w_vals = [w[...] if w.dtype == BF16 else w[...].astype(BF16) for w in w_refs]
    a_vals = {}
    accs = []
    for ai, wi in dots:
        if ai not in a_vals:
            a_vals[ai] = a_refs[ai][...].astype(BF16)
        accs.append(jnp.dot(a_vals[ai], w_vals[wi], preferred_element_type=F32))
    o_ref[...] = epilogue(accs, [e[...] for e in e_refs]).astype(o_ref.dtype)
    for w_out, w_val in zip(w_out_refs, w_vals):
        w_out[...] = w_val


class _LayerWeight:
    def __init__(self, stack, layer):
        self.stack, self.layer = stack, layer
        self.shape, self.dtype = stack.shape[1:], stack.dtype
        self.size = self.shape[0] * self.shape[1]

    def block_spec(self, tn):
        return pl.BlockSpec((None, self.shape[0], tn), lambda i, j: (self.layer, 0, j))


def _mm_vmem(tm, tn, a_list, w_list, n_dots, tiles, out_dtype, emit_weights):
    a_bytes = sum(tm * a.shape[1] * a.dtype.itemsize for a in a_list)
    w_bytes = sum(w.shape[0] * tn * (w.dtype.itemsize + (2 if emit_weights else 0)) for w in w_list)
    io_bytes = tm * tn * (jnp.dtype(out_dtype).itemsize + 4 * tiles)
    return 2 * (a_bytes + w_bytes + io_bytes) + (n_dots + 1) * tm * tn * 4


def _fused_matmul(a_list, w_list, dots, extras, epilogue, out_dtype, name, emit_weights=False):
    m, n = a_list[0].shape[0], w_list[0].shape[1]
    n_tiles = sum(1 for e in extras if e.shape[0] != 1)
    flops = 2.0 * m * n * sum(w_list[wi].shape[0] for _, wi in dots)
    a_bytes = sum(a.size * a.dtype.itemsize for a in a_list)
    w_bytes = sum(w.size * (w.dtype.itemsize + (2 if emit_weights else 0)) for w in w_list)
    io_bytes = m * n * (jnp.dtype(out_dtype).itemsize + 4 * n_tiles)
    choice = None
    for tm in (1024, 512, 256, 128, 64, 32, 16, 8):
        for tn in (1024, 512, 256, 128):
            if m % tm or n % tn or (emit_weights and tm != m):
                continue
            est = _mm_vmem(tm, tn, a_list, w_list, len(dots), n_tiles, out_dtype, emit_weights)
            if est > MM_VMEM_BUDGET:
                continue
            t_mxu = flops / MXU_FLOPS * (1.0 if tn % MXU_WIDTH == 0 and tm % MXU_WIDTH == 0 else 2.0)
            t_hbm = (a_bytes + (m // tm) * w_bytes + io_bytes) / HBM_BYTES_PER_S
            cost = max(t_mxu, t_hbm) + (m // tm) * (n // tn) * GRID_STEP_S
            if choice is None or cost < choice[0]:
                choice = (cost, tm, tn, est)
    if choice is None:
        raise ValueError(f"{name}: no matmul tiling fits VMEM")
    _, tm, tn, est = choice
    in_specs = [pl.BlockSpec((tm, a.shape[1]), lambda i, j: (i, 0)) for a in a_list]
    in_specs += [w.block_spec(tn) if isinstance(w, _LayerWeight) else pl.BlockSpec((w.shape[0], tn), lambda i, j: (0, j))
                 for w in w_list]
    for e in extras:
        if e.shape[0] == 1:
            in_specs.append(pl.BlockSpec((1, tn), lambda i, j: (0, j)))
        else:
            in_specs.append(pl.BlockSpec((tm, tn), lambda i, j: (i, j)))
    body = functools.partial(_mm_body, n_a=len(a_list), n_w=len(w_list), n_e=len(extras), dots=tuple(dots),
                             epilogue=epilogue)
    out_shape = [jax.ShapeDtypeStruct((m, n), out_dtype)]
    out_specs = [pl.BlockSpec((tm, tn), lambda i, j: (i, j))]
    if emit_weights:
        assert m == tm
        out_shape += [jax.ShapeDtypeStruct(w.shape, BF16) for w in w_list]
        out_specs += [pl.BlockSpec((w.shape[0], tn), lambda i, j: (0, j)) for w in w_list]
    outs = pl.pallas_call(
        body,
        out_shape=out_shape,
        grid=(m // tm, n // tn),
        in_specs=in_specs,
        out_specs=out_specs,
        compiler_params=_params(("parallel", "parallel"), est),
        name=name,
    )(*a_list, *[w.stack if isinstance(w, _LayerWeight) else w for w in w_list], *extras)
    return (outs[0], outs[1:]) if emit_weights else outs[0]


def _ep_plain(accs, ex):
    return accs[0]


def _ep_swiglu(accs, ex):
    return _silu(accs[0]) * accs[1]


def _ep_half_residual(accs, ex):
    return ex[0] + 0.5 * accs[0]


def _ep_residual(accs, ex):
    return ex[0] + accs[0]


def _ep_residual2(accs, ex):
    return ex[0] + (accs[0] + accs[1])


def _ep_glu(accs, ex):
    return ex[0] * jax.nn.sigmoid(accs[0] + ex[1])


def _ep_gated_embed(accs, ex):
    return ex[0] + jax.nn.sigmoid(accs[0]) * accs[1]


def _ssd_body(z_ref, xs_ref, bm_ref, cm_ref, dtc_ref, dtp_ref, hx_ref, hb_ref, hc_ref, h0_ref,
              wx_ref, wb_ref, wc_ref, bx_ref, bb_ref, bc_ref, alc_ref, alp_ref, dbc_ref, dbp_ref, drep_ref, ng_ref,
              ya_ref, hfin_ref, state_sc, xp_sc, *, heads, head_dim, n_state, chunk, conv_k):
    c = pl.program_id(2)
    q, p, n = chunk, head_dim, n_state
    w = heads * p
    t0 = SUBLANES - (conv_k - 1)

    @pl.when(c == 0)
    def _load_carried():
        state_sc[...] = h0_ref[...]
        xp_sc[t0:SUBLANES, 0:w] = hx_ref[...]
        xp_sc[t0:SUBLANES, w:w + n] = hb_ref[...]
        xp_sc[t0:SUBLANES, w + n:w + 2 * n] = hc_ref[...]

    xp_sc[SUBLANES:SUBLANES + q, 0:w] = xs_ref[...]
    xp_sc[SUBLANES:SUBLANES + q, w:w + n] = bm_ref[...]
    xp_sc[SUBLANES:SUBLANES + q, w + n:w + 2 * n] = cm_ref[...]

    def conv_silu(lo, hi, w_ref, b_ref):
        acc = xp_sc[t0:t0 + q, lo:hi] * w_ref[0:1, :]
        for j in range(1, conv_k):
            acc = acc + xp_sc[t0 + j:t0 + j + q, lo:hi] * w_ref[j:j + 1, :]
        return _silu(acc + b_ref[...])

    xs = conv_silu(0, w, wx_ref, bx_ref)
    bm = conv_silu(w, w + n, wb_ref, bb_ref)
    cm = conv_silu(w + n, w + 2 * n, wc_ref, bc_ref)
    xp_sc[t0:SUBLANES, :] = xp_sc[SUBLANES + q - (conv_k - 1):SUBLANES + q, :]

    dt_c = jax.nn.softplus(dtc_ref[...] + dbc_ref[...])
    dt_p = jax.nn.softplus(dtp_ref[...] + dbp_ref[...])
    ii = lax.broadcasted_iota(jnp.int32, (q, q), 0)
    jj = lax.broadcasted_iota(jnp.int32, (q, q), 1)
    acum_c = jnp.dot((ii >= jj).astype(F32), dt_c * -jnp.exp(alc_ref[...]), precision=HIGHEST,
                     preferred_element_type=F32)
    i2 = lax.broadcasted_iota(jnp.int32, (2 * q, 2 * q), 0)
    j2 = lax.broadcasted_iota(jnp.int32, (2 * q, 2 * q), 1)
    same_head = (i2 >= q) == (j2 >= q)
    acum_p = jnp.dot(dt_p * -jnp.exp(alp_ref[...]), ((i2 <= j2) & same_head).astype(F32), precision=HIGHEST,
                     preferred_element_type=F32)

    lane = lax.broadcasted_iota(jnp.int32, (q, LANES), 1)
    first_head = lane < p

    def per_head_lanes(v):
        return jnp.concatenate([jnp.where(first_head, v[:, 2 * k:2 * k + 1], v[:, 2 * k + 1:2 * k + 2])
                                for k in range(heads // 2)], axis=1)

    acum_w = per_head_lanes(acum_c)
    grow_w = jnp.exp(acum_w)
    wend_w = jnp.exp(acum_w[q - 1:q, :] - acum_w) * per_head_lanes(dt_c)

    cm_b = cm.astype(BF16)
    bm_b = bm.astype(BF16)
    scores2 = lax.dot_general(cm_b, jnp.concatenate([bm_b, bm_b], axis=0), NT_DIMS,
                              preferred_element_type=F32)
    row = lax.broadcasted_iota(jnp.int32, (q, 2 * q), 0)
    col2 = lax.broadcasted_iota(jnp.int32, (q, 2 * q), 1)
    causal2 = row >= jnp.where(col2 >= q, col2 - q, col2)
    y_pairs = []
    for k in range(heads // 2):
        seg = acum_w[:, k * LANES:(k + 1) * LANES] - acum_p[k:k + 1, :]
        decay = jnp.exp(jnp.where(causal2, seg, -jnp.inf))
        m2 = (scores2 * decay * dt_p[k:k + 1, :]).astype(BF16)
        x_pair = xs[:, k * LANES:(k + 1) * LANES]
        x2 = jnp.concatenate([jnp.where(first_head, x_pair, 0.0), jnp.where(first_head, 0.0, x_pair)], axis=0)
        y_pairs.append(jnp.dot(m2, x2.astype(BF16), preferred_element_type=F32))
    st = state_sc[...]
    y_off = jnp.dot(cm_b, st.astype(BF16), preferred_element_type=F32)
    xw = (xs * wend_w).astype(BF16)
    state_sc[...] = st * grow_w[q - 1:q, :] + lax.dot_general(bm_b, xw, TN_DIMS, preferred_element_type=F32)

    y = jnp.concatenate(y_pairs, axis=1) + y_off * grow_w + drep_ref[...] * xs
    y = y * _silu(z_ref[...])
    inv = lax.rsqrt(jnp.mean(y * y, axis=-1, keepdims=True) + EPS)
    ya_ref[...] = (y * inv * ng_ref[...]).astype(ya_ref.dtype)

    @pl.when(c == pl.num_programs(2) - 1)
    def _store_state():
        hfin_ref[...] = state_sc[...]


def _ssd_mixer(zx, dt_raw, conv_hist, h0, conv_w, conv_b, a_log, dt_bias, d_skip, norm_g):
    b, l, _ = zx.shape
    _, heads, p, n = h0.shape
    inner = heads * p
    conv_k, conv_dim = conv_w.shape
    g = (conv_dim - inner) // (2 * n)
    r = heads // g
    w = r * p
    q = SSD_CHUNK if l % SSD_CHUNK == 0 else l
    nc = l // q
    assert 2 * p == LANES and r % 2 == 0 and 2 * q == LANES, "two heads / two chunk copies per 128-lane tile"
    dt5 = dt_raw.reshape(b, nc, q, g, r)
    dtc = dt5.transpose(0, 3, 1, 2, 4)
    dtp = dt5.transpose(0, 3, 1, 4, 2).reshape(b, g, nc, r // 2, 2 * q)
    pair_lanes = lambda v: jnp.repeat(v.reshape(g, r // 2, 2), q, axis=-1)
    state0 = h0.reshape(b, g, r, p, n).transpose(0, 1, 4, 2, 3).reshape(b, g, n, w)
    conv_b2 = conv_b.reshape(1, conv_dim)
    nb0 = inner // n

    def col(block_w, first):
        return pl.BlockSpec((None, q, block_w), lambda bi, gi, ci: (bi, ci, first + gi))

    def hist(block_w, first):
        return pl.BlockSpec((None, conv_k - 1, block_w), lambda bi, gi, ci: (bi, 0, first + gi))

    def wrow(rows, block_w, first):
        return pl.BlockSpec((rows, block_w), lambda bi, gi, ci: (0, first + gi))

    def per_group(shape):
        return pl.BlockSpec((None,) + shape, lambda bi, gi, ci: (gi, 0, 0))

    in_specs = [
        col(w, 0), col(w, inner // w), col(n, 2 * inner // n), col(n, 2 * inner // n + g),
        pl.BlockSpec((None, None, None, q, r), lambda bi, gi, ci: (bi, gi, ci, 0, 0)),
        pl.BlockSpec((None, None, None, r // 2, 2 * q), lambda bi, gi, ci: (bi, gi, ci, 0, 0)),
        hist(w, 0), hist(n, nb0), hist(n, nb0 + g),
        pl.BlockSpec((None, None, n, w), lambda bi, gi, ci: (bi, gi, 0, 0)),
        wrow(conv_k, w, 0), wrow(conv_k, n, nb0), wrow(conv_k, n, nb0 + g),
        wrow(1, w, 0), wrow(1, n, nb0), wrow(1, n, nb0 + g),
        per_group((1, r)), per_group((r // 2, 2 * q)), per_group((1, r)), per_group((r // 2, 2 * q)),
        per_group((1, w)), per_group((1, w)),
    ]
    out_specs = [
        pl.BlockSpec((None, q, w), lambda bi, gi, ci: (bi, ci, gi)),
        pl.BlockSpec((None, None, n, w), lambda bi, gi, ci: (bi, gi, 0, 0)),
    ]
    est = 4 * (2 * q * (2 * w + 2 * n) + 6 * n * w + (SUBLANES + q) * (w + 2 * n) + 12 * q * w)
    body = functools.partial(_ssd_body, heads=r, head_dim=p, n_state=n, chunk=q, conv_k=conv_k)
    ya, state = pl.pallas_call(
        body,
        out_shape=[jax.ShapeDtypeStruct((b, l, inner), BF16), jax.ShapeDtypeStruct((b, g, n, w), F32)],
        grid=(b, g, nc),
        in_specs=in_specs,
        out_specs=out_specs,
        scratch_shapes=[pltpu.VMEM((n, w), F32), pltpu.VMEM((SUBLANES + q, w + 2 * n), F32)],
        compiler_params=_params(("parallel", "parallel", "arbitrary"), est),
        name="ssd_mixer",
    )(zx, zx, zx, zx, dtc, dtp, conv_hist, conv_hist, conv_hist, state0,
      conv_w, conv_w, conv_w, conv_b2, conv_b2, conv_b2,
      a_log.reshape(g, 1, r), pair_lanes(a_log), dt_bias.reshape(g, 1, r), pair_lanes(dt_bias),
      jnp.repeat(d_skip, p).reshape(g, 1, w), norm_g.reshape(g, 1, w))
    h_fin = state.reshape(b, g, n, r, p).transpose(0, 1, 3, 4, 2).reshape(b, heads, p, n)
    return ya, h_fin


def _gelu_tanh(x):
    return 0.5 * x * (1.0 + jnp.tanh(0.7978845608028654 * (x + 0.044715 * (x * x * x))))


def _s5_body(u_ref, h0re_ref, h0im_ref, lam_ref, bblk_ref, cblk_ref, d_ref,
             y_ref, ybf_ref, hre_ref, him_ref, bu_sc, h_sc, *, n_seq, n_pack, t_blk, pitch, n_tiles, unroll):
    j = pl.program_id(1)
    half = n_tiles // 2
    rows = n_seq * t_blk
    cw = u_ref.shape[-1] // n_pack
    sw = half * LANES

    @pl.when(j == 0)
    def _load_state():
        for pk in range(n_pack):
            for k in range(half):
                cols = slice(pk * sw + k * LANES, pk * sw + (k + 1) * LANES)
                h_sc[k, pk * n_seq:(pk + 1) * n_seq, :] = h0re_ref[:, cols]
                h_sc[half + k, pk * n_seq:(pk + 1) * n_seq, :] = h0im_ref[:, cols]

    for pk in range(n_pack):
        u_b = u_ref[:, :, pk * cw:(pk + 1) * cw].reshape(rows, cw).astype(BF16)
        for kk in range(half):
            bu = jnp.dot(u_b, bblk_ref[pk, :, kk * 2 * LANES:(kk + 1) * 2 * LANES], preferred_element_type=F32)
            for s in range(n_seq):
                base = (pk * n_seq + s) * pitch
                bu_sc[2 * kk, base:base + t_blk, :] = bu[s * t_blk:(s + 1) * t_blk, :LANES]
                bu_sc[2 * kk + 1, base:base + t_blk, :] = bu[s * t_blk:(s + 1) * t_blk, LANES:]

    def lam_rows(k):
        return jnp.concatenate([jnp.broadcast_to(lam_ref[pk, :, k * LANES:(k + 1) * LANES], (n_seq, LANES))
                                for pk in range(n_pack)], axis=0)

    lam_re = [lam_rows(k) for k in range(half)]
    lam_im = [lam_rows(half + k) for k in range(half)]

    def step(t, hs):
        idx = pl.ds(t, n_pack * n_seq, stride=pitch)
        new_re, new_im = [], []
        for k in range(half):
            h_re, h_im = hs[k], hs[half + k]
            re_rows, im_rows = bu_sc.at[k], bu_sc.at[half + k]
            n_re = lam_re[k] * h_re - lam_im[k] * h_im + re_rows[idx, :]
            n_im = lam_re[k] * h_im + lam_im[k] * h_re + im_rows[idx, :]
            re_rows[idx, :] = n_re
            im_rows[idx, :] = n_im
            new_re.append(n_re)
            new_im.append(n_im)
        return tuple(new_re + new_im)

    hs = lax.fori_loop(0, t_blk, step, tuple(h_sc[k] for k in range(n_tiles)), unroll=unroll)
    for k in range(n_tiles):
        h_sc[k] = hs[k]

    for pk in range(n_pack):
        h_all = jnp.concatenate(
            [jnp.concatenate([bu_sc[k, (pk * n_seq + s) * pitch:(pk * n_seq + s) * pitch + t_blk, :].astype(BF16)
                              for s in range(n_seq)], axis=0) for k in range(n_tiles)], axis=1)
        u = u_ref[:, :, pk * cw:(pk + 1) * cw].reshape(rows, cw)
        y = jnp.dot(h_all, cblk_ref[pk], preferred_element_type=F32) + d_ref[:, pk * cw:(pk + 1) * cw] * u
        y = _gelu_tanh(y)
        y_ref[:, :, pk * cw:(pk + 1) * cw] = y.reshape(n_seq, t_blk, cw)
        ybf_ref[:, :, pk * cw:(pk + 1) * cw] = y.astype(BF16).reshape(n_seq, t_blk, cw)

    @pl.when(j == pl.num_programs(1) - 1)
    def _store_state():
        for pk in range(n_pack):
            for k in range(half):
                cols = slice(pk * sw + k * LANES, pk * sw + (k + 1) * LANES)
                hre_ref[:, cols] = hs[k][pk * n_seq:(pk + 1) * n_seq, :]
                him_ref[:, cols] = hs[half + k][pk * n_seq:(pk + 1) * n_seq, :]


def _s5_discretize(lam_re, lam_im, log_step, b_re, b_im, c_re, c_im):
    g, p, k = b_re.shape
    gb = S5_GROUP_BLOCK
    nb = g // gb
    step = jnp.exp(log_step)[:, None]
    mag = jnp.exp(lam_re * step)
    lb_re, lb_im = mag * jnp.cos(lam_im * step), mag * jnp.sin(lam_im * step)
    den = lam_re * lam_re + lam_im * lam_im
    f_re = ((lb_re - 1.0) * lam_re + lb_im * lam_im) / den
    f_im = (lb_im * lam_re - (lb_re - 1.0) * lam_im) / den
    bb_re = f_re[..., None] * b_re - f_im[..., None] * b_im
    bb_im = f_re[..., None] * b_im + f_im[..., None] * b_re
    eye = jnp.eye(gb, dtype=F32)
    lam = jnp.stack([lb_re.reshape(nb, gb * p), lb_im.reshape(nb, gb * p)], axis=1).reshape(nb, 1, 2 * gb * p)
    bb = jnp.stack([bb_re, bb_im]).reshape(2, nb, gb, p, k).transpose(1, 2, 4, 0, 3)
    bblk = (bb[:, :, :, :, None, :] * eye[None, :, None, None, :, None]).reshape(nb, gb * k, 2 * gb * p)
    cc = jnp.stack([c_re, -c_im]).reshape(2, nb, gb, k, p).transpose(1, 0, 2, 4, 3)
    cblk = (cc[:, :, :, :, None, :] * eye[None, None, :, None, :, None]).reshape(nb, 2 * gb * p, gb * k)
    return lam, bblk.astype(BF16), cblk.astype(BF16)


def _s5_mixer(u, h0_re, h0_im, lam, bblk, cblk, d_skip):
    b, l, width = u.shape
    _, g, p = h0_re.shape
    nb, cw, sw2 = bblk.shape
    n_tiles = sw2 // LANES
    n_pack = max(1, SUBLANES // b)
    if nb % n_pack:
        n_pack = 1
    t_blk = _largest_divisor(l, (256, 128, 64, 32, 16, 8))
    rows = n_pack * b * t_blk
    pitch = SUBLANES * (t_blk // SUBLANES + (1 - (t_blk // SUBLANES) % 2))
    est = 6 * rows * sw2 + 2 * rows * cw * 10 + 8 * n_pack * cw * sw2 + 6 * rows * cw * 4
    body = functools.partial(_s5_body, n_seq=b, n_pack=n_pack, t_blk=t_blk, pitch=pitch, n_tiles=n_tiles,
                             unroll=S5_SCAN_UNROLL)
    seq_spec = pl.BlockSpec((b, t_blk, n_pack * cw), lambda gi, ti: (0, ti, gi))
    state_spec = pl.BlockSpec((b, n_pack * sw2 // 2), lambda gi, ti: (0, gi))
    y, ybf, h_re, h_im = pl.pallas_call(
        body,
        out_shape=[jax.ShapeDtypeStruct((b, l, width), F32), jax.ShapeDtypeStruct((b, l, width), BF16),
                   jax.ShapeDtypeStruct((b, g * p), F32), jax.ShapeDtypeStruct((b, g * p), F32)],
        grid=(nb // n_pack, l // t_blk),
        in_specs=[seq_spec, state_spec, state_spec,
                  pl.BlockSpec((n_pack, 1, sw2), lambda gi, ti: (gi, 0, 0)),
                  pl.BlockSpec((n_pack, cw, sw2), lambda gi, ti: (gi, 0, 0)),
                  pl.BlockSpec((n_pack, sw2, cw), lambda gi, ti: (gi, 0, 0)),
                  pl.BlockSpec((1, n_pack * cw), lambda gi, ti: (0, gi))],
        out_specs=[seq_spec, seq_spec, state_spec, state_spec],
        scratch_shapes=[pltpu.VMEM((n_tiles, n_pack * b * pitch, LANES), F32),
                        pltpu.VMEM((n_tiles, n_pack * b, LANES), F32)],
        compiler_params=_params(("parallel", "arbitrary"), est),
        name="s5_mixer",
    )(u, h0_re.reshape(b, g * p), h0_im.reshape(b, g * p), lam, bblk, cblk, d_skip.reshape(1, width))
    return y, ybf, h_re.reshape(b, g, p), h_im.reshape(b, g, p)


def _short_conv_body(b_ref, c_ref, v_ref, hist_ref, w_ref, o_ref, hnew_ref, cv_sc, *, t_blk, conv_k):
    t = pl.program_id(2)
    t0 = SUBLANES - (conv_k - 1)

    @pl.when(t == 0)
    def _load_hist():
        cv_sc[t0:SUBLANES, :] = hist_ref[...]

    cv_sc[SUBLANES:SUBLANES + t_blk, :] = c_ref[...] * v_ref[...]
    acc = cv_sc[t0:t0 + t_blk, :] * w_ref[0:1, :]
    for j in range(1, conv_k):
        acc = acc + cv_sc[t0 + j:t0 + j + t_blk, :] * w_ref[j:j + 1, :]
    o_ref[...] = (b_ref[...] * acc).astype(o_ref.dtype)
    cv_sc[t0:SUBLANES, :] = cv_sc[SUBLANES + t_blk - (conv_k - 1):SUBLANES + t_blk, :]

    @pl.when(t == pl.num_programs(2) - 1)
    def _store_hist():
        hnew_ref[...] = cv_sc[t0:SUBLANES, :]


def _short_conv(bcv, hist, conv_w):
    bsz, l, width3 = bcv.shape
    width = width3 // 3
    conv_k = conv_w.shape[0]
    t_blk = _largest_divisor(l, (256, 128, 64, 32, 16, 8))
    tw = _largest_divisor(width, (1024, 512, 256, 128))
    nw = width // tw

    def col(first):
        return pl.BlockSpec((None, t_blk, tw), lambda bi, wi, ti: (bi, ti, first + wi))

    body = functools.partial(_short_conv_body, t_blk=t_blk, conv_k=conv_k)
    return pl.pallas_call(
        body,
        out_shape=[jax.ShapeDtypeStruct((bsz, l, width), BF16), jax.ShapeDtypeStruct(hist.shape, F32)],
        grid=(bsz, nw, l // t_blk),
        in_specs=[col(0), col(nw), col(2 * nw),
                  pl.BlockSpec((None, conv_k - 1, tw), lambda bi, wi, ti: (bi, 0, wi)),
                  pl.BlockSpec((conv_k, tw), lambda bi, wi, ti: (0, wi))],
        out_specs=[pl.BlockSpec((None, t_blk, tw), lambda bi, wi, ti: (bi, ti, wi)),
                   pl.BlockSpec((None, conv_k - 1, tw), lambda bi, wi, ti: (bi, 0, wi))],
        scratch_shapes=[pltpu.VMEM((SUBLANES + t_blk, tw), F32)],
        compiler_params=_params(("parallel", "parallel", "arbitrary"), 4 * t_blk * tw * 12),
        name="short_conv",
    )(bcv, bcv, bcv, hist, conv_w)


def _prepare_weights(w):
    heads, p, _ = w["ssd_shape"]
    inner = heads * p
    conv_dim = w["ssd_conv_w"].shape[-1]
    out = {}
    w_in0 = w["w_in0"]
    s1 = inner + conv_dim
    out["w_zx"] = w_in0[:, :, :s1].astype(BF16)
    out["w_dt"] = jnp.pad(w_in0[:, :, s1:s1 + heads].astype(BF16), ((0, 0), (0, 0), (0, _round_up(heads, LANES) - heads)))
    out["w_u"] = w_in0[:, :, s1 + heads:].astype(BF16)
    out["w_out0_a"] = w["w_out0"][:, :inner].astype(BF16)
    out["w_out0_b"] = w["w_out0"][:, inner:].astype(BF16)
    out["s5"] = [_s5_discretize(w["s5_lam_re"][j], w["s5_lam_im"][j], w["s5_log_step"][j], w["s5_b_re"][j],
                                w["s5_b_im"][j], w["s5_c_re"][j], w["s5_c_im"][j])
                 for j in range(w["s5_lam_re"].shape[0])]
    return out


def _run_trunk(x3, p4, conv_ssd, h_ssd, s5_re, s5_im, conv_sc, w, wb, emitted):
    bsz, l, d = x3.shape
    m = bsz * l
    depth = w["g_ffn1"].shape[0]
    heads = h_ssd.shape[2]
    x = x3.reshape(m, d)

    def mm(a_list, names, i, dots, extras, epilogue, out_dtype, tag):
        if names[0] in wb:
            ws = [wb[nm][i] if isinstance(wb[nm], dict) else _LayerWeight(wb[nm], i) for nm in names]
            return _fused_matmul(a_list, ws, dots, extras, epilogue, out_dtype, tag)
        out, copies = _fused_matmul(a_list, [_LayerWeight(w[nm], i) for nm in names], dots, extras, epilogue,
                                    out_dtype, tag, emit_weights=True)
        for nm, copy in zip(names, copies):
            emitted.setdefault(nm, {})[i] = copy
        return out

    def ffn_half_step(x, g, i, tag):
        xn = _rmsnorm(x, g, BF16)
        h = mm([xn], [f"w_{tag}_gate", f"w_{tag}_up"], i, [(0, 0), (0, 1)], [], _ep_swiglu, BF16, f"{tag}_gate_up")
        return mm([h], [f"w_{tag}_down"], i, [(0, 0)], [x], _ep_half_residual, F32, f"{tag}_down")

    out_conv_ssd, out_h, out_re, out_im, out_sc = [], [], [], [], []
    for i in range(depth):
        x = ffn_half_step(x, w["g_ffn1"][i], i, "ffn1")
        xn = _rmsnorm(x, w["g_mix"][i], BF16)
        j = i // 2
        if i % 2 == 0:
            zx = mm([xn], ["w_zx"], j, [(0, 0)], [], _ep_plain, F32, "in0_zxbc")
            dt = mm([xn], ["w_dt"], j, [(0, 0)], [], _ep_plain, F32, "in0_dt")[:, :heads]
            u = mm([xn], ["w_u"], j, [(0, 0)], [], _ep_plain, F32, "in0_u")
            zx3 = zx.reshape(bsz, l, zx.shape[-1])
            ya, h_new = _ssd_mixer(zx3, dt.reshape(bsz, l, heads), conv_ssd[j], h_ssd[j], w["ssd_conv_w"][j],
                                   w["ssd_conv_b"][j], w["ssd_a_log"][j], w["ssd_dt_bias"][j], w["ssd_d"][j],
                                   w["ssd_norm_g"][j])
            inner = ya.shape[-1]
            conv_k, conv_dim = w["ssd_conv_w"][j].shape
            xbc_tail = zx3[:, l - min(l, conv_k - 1):, inner:inner + conv_dim]
            c_new = jnp.concatenate([conv_ssd[j], xbc_tail], axis=1)[:, -(conv_k - 1):]
            lam, bblk, cblk = wb["s5"][j]
            yb, yb_bf, re_new, im_new = _s5_mixer(u.reshape(bsz, l, u.shape[-1]), s5_re[j], s5_im[j], lam, bblk, cblk,
                                                  w["s5_d"][j])
            width = yb.shape[-1]
            yb_glu = mm([yb_bf.reshape(m, width)], ["s5_w_glu"], j, [(0, 0)],
                        [yb.reshape(m, width), w["s5_b_glu"][j].reshape(1, width)], _ep_glu, BF16, "s5_glu")
            x = mm([ya.reshape(m, inner), yb_glu], ["w_out0_a", "w_out0_b"], j, [(0, 0), (1, 1)], [x],
                   _ep_residual2, F32, "out0")
            out_conv_ssd.append(c_new)
            out_h.append(h_new)
            out_re.append(re_new)
            out_im.append(im_new)
        else:
            bcv = mm([xn], ["w_in1"], j, [(0, 0)], [], _ep_plain, F32, "in1")
            yc, sc_new = _short_conv(bcv.reshape(bsz, l, bcv.shape[-1]), conv_sc[j], w["sc_conv_w"][j])
            x = mm([yc.reshape(m, yc.shape[-1])], ["w_out1"], j, [(0, 0)], [x], _ep_residual, F32, "out1")
            out_sc.append(sc_new)
        x = ffn_half_step(x, w["g_ffn2"][i], i, "ffn2")
        xn = _rmsnorm(x, w["g_pe"][i], BF16)
        x = mm([xn, p4[i].reshape(m, p4.shape[-1])], ["w_pe_gate", "w_pe_proj"], i, [(0, 0), (1, 1)], [x],
               _ep_gated_embed, F32, "pe_embed")
    y = _rmsnorm(x, w["g_final"], F32).reshape(bsz, l, d)
    return (y, jnp.stack(out_conv_ssd), jnp.stack(out_h), jnp.stack(out_re), jnp.stack(out_im), jnp.stack(out_sc))


def kernel(x_prompt, x_sample, p_prompt, p_sample, cache_conv_ssd, state_ssd, state_s5_re, state_s5_im, cache_conv_short, g_ffn1, w_ffn1_gate, w_ffn1_up, w_ffn1_down, g_mix, g_ffn2, w_ffn2_gate, w_ffn2_up, w_ffn2_down, g_pe, w_pe_gate, w_pe_proj, w_in0, ssd_conv_w, ssd_conv_b, ssd_a_log, ssd_dt_bias, ssd_d, ssd_norm_g, s5_lam_re, s5_lam_im, s5_log_step, s5_b_re, s5_b_im, s5_c_re, s5_c_im, s5_d, s5_w_glu, s5_b_glu, w_out0, w_in1, sc_conv_w, w_out1, g_final):
    w = dict(g_ffn1=g_ffn1, w_ffn1_gate=w_ffn1_gate, w_ffn1_up=w_ffn1_up, w_ffn1_down=w_ffn1_down,
             g_mix=g_mix, g_ffn2=g_ffn2, w_ffn2_gate=w_ffn2_gate, w_ffn2_up=w_ffn2_up, w_ffn2_down=w_ffn2_down,
             g_pe=g_pe, w_pe_gate=w_pe_gate, w_pe_proj=w_pe_proj, w_in0=w_in0, ssd_conv_w=ssd_conv_w,
             ssd_conv_b=ssd_conv_b, ssd_a_log=ssd_a_log, ssd_dt_bias=ssd_dt_bias, ssd_d=ssd_d,
             ssd_norm_g=ssd_norm_g, s5_lam_re=s5_lam_re, s5_lam_im=s5_lam_im, s5_log_step=s5_log_step,
             s5_b_re=s5_b_re, s5_b_im=s5_b_im, s5_c_re=s5_c_re, s5_c_im=s5_c_im, s5_d=s5_d,
             s5_w_glu=s5_w_glu, s5_b_glu=s5_b_glu, w_out0=w_out0, w_in1=w_in1, sc_conv_w=sc_conv_w,
             w_out1=w_out1, g_final=g_final, ssd_shape=state_ssd.shape[2:])
    wb = _prepare_weights(w)
    emitted = {}
    sample = _run_trunk(x_sample, p_sample, cache_conv_ssd, state_ssd, state_s5_re, state_s5_im, cache_conv_short, w,
                        wb, emitted)
    wb.update(emitted)
    b = x_prompt.shape[0]
    zeros = lambda ref: jnp.zeros((ref.shape[0], b) + ref.shape[2:], F32)
    prompt = _run_trunk(x_prompt, p_prompt, zeros(cache_conv_ssd), zeros(state_ssd), zeros(state_s5_re),
                        zeros(state_s5_im), zeros(cache_conv_short), w, wb, None)
    return (prompt[0], sample[0]) + prompt[1:] + sample[1:]
```

```python
import functools

import jax
import jax.numpy as jnp
from jax import lax
from jax.experimental import pallas as pl
from jax.experimental.pallas import tpu as pltpu

F32 = jnp.float32
BF16 = jnp.bfloat16
EPS = 1e-6
SSD_CHUNK = 64
S5_GROUP_BLOCK = 16
S5_SCAN_UNROLL = 8
LANES = 128
SUBLANES = 8
V7X_VMEM_BYTES = 64 * 1024 * 1024
MM_VMEM_BUDGET = 42 * 1024 * 1024
MXU_WIDTH = 256
MXU_FLOPS = 0.95e15
HBM_BYTES_PER_S = 3.0e12
GRID_STEP_S = 0.35e-6
HIGHEST = lax.Precision.HIGHEST
NT_DIMS = (((1,), (1,)), ((), ()))
TN_DIMS = (((0,), (0,)), ((), ()))


def _largest_divisor(n, candidates):
    for c in candidates:
        if n % c == 0:
            return c
    raise ValueError(f"no tile in {candidates} divides {n}")


def _round_up(n, m):
    return (n + m - 1) // m * m


def _params(semantics, vmem_estimate):
    limit = min(V7X_VMEM_BYTES - 8 * 1024 * 1024, int(vmem_estimate * 1.25) + 8 * 1024 * 1024)
    return pltpu.CompilerParams(dimension_semantics=semantics, vmem_limit_bytes=limit)


def _silu(x):
    return x * jax.nn.sigmoid(x)


def _rmsnorm_body(x_ref, g_ref, o_ref):
    x = x_ref[...]
    inv = lax.rsqrt(jnp.mean(x * x, axis=-1, keepdims=True) + EPS)
    o_ref[...] = (x * inv * g_ref[...]).astype(o_ref.dtype)


def _rmsnorm(x, g, out_dtype):
    m, d = x.shape
    tm = _largest_divisor(m, (256, 128, 64, 32, 16, 8))
    return pl.pallas_call(
        _rmsnorm_body,
        out_shape=jax.ShapeDtypeStruct((m, d), out_dtype),
        grid=(m // tm,),
        in_specs=[pl.BlockSpec((tm, d), lambda i: (i, 0)), pl.BlockSpec((1, d), lambda i: (0, 0))],
        out_specs=pl.BlockSpec((tm, d), lambda i: (i, 0)),
        compiler_params=_params(("parallel",), 2 * tm * d * (4 + jnp.dtype(out_dtype).itemsize) + 3 * tm * d * 4),
        name="rmsnorm",
    )(x, g.reshape(1, d))


def _mm_body(*refs, n_a, n_w, n_e, dots, epilogue):
    a_refs, w_refs, e_refs = refs[:n_a], refs[n_a:n_a + n_w], refs[n_a + n_w:n_a + n_w + n_e]
    o_ref, w_out_refs = refs[n_a + n_w + n_e], refs[n_a + n_w + n_e + 1:]
    w_vals = [w[...] if w.dtype == BF16 else w[...].astype(BF16) for w in w_refs]
    a_vals = {}
    accs = []
    for ai, wi in dots:
        if ai not in a_vals:
            a_vals[ai] = a_refs[ai][...].astype(BF16)
        accs.append(jnp.dot(a_vals[ai], w_vals[wi], preferred_element_type=F32))
    o_ref[...] = epilogue(accs, [e[...] for e in e_refs]).astype(o_ref.dtype)
    for w_out, w_val in zip(w_out_refs, w_vals):
        w_out[...] = w_val


class _LayerWeight:
    def __init__(self, stack, layer, k=None, row_block=0, col_start=0, n=None):
        self.stack, self.layer, self.row_block, self.col_start = stack, layer, row_block, col_start
        self.shape = (k or stack.shape[1], n or stack.shape[2])
        self.dtype = stack.dtype
        self.size = self.shape[0] * self.shape[1]

    def allows(self, tn):
        return self.col_start % tn == 0

    def block_spec(self, tn):
        first = self.col_start // tn
        return pl.BlockSpec((None, self.shape[0], tn), lambda i, j: (self.layer, self.row_block, first + j))


def _mm_vmem(tm, tn, a_list, w_list, n_dots, tiles, out_dtype, emit_weights, a_buffers):
    a_bytes = sum(tm * a.shape[1] * a.dtype.itemsize for a in a_list)
    w_bytes = sum(w.shape[0] * tn * (w.dtype.itemsize + (2 if emit_weights else 0)) for w in w_list)
    io_bytes = tm * tn * (jnp.dtype(out_dtype).itemsize + 4 * tiles)
    return a_buffers * a_bytes + 2 * (w_bytes + io_bytes) + (n_dots + 1) * tm * tn * 4


def _fused_matmul(a_list, w_list, dots, extras, epilogue, out_dtype, name, emit_weights=False):
    m, n = a_list[0].shape[0], w_list[0].shape[1]
    n_tiles = sum(1 for e in extras if e.shape[0] != 1)
    flops = 2.0 * m * n * sum(w_list[wi].shape[0] for _, wi in dots)
    a_bytes = sum(a.size * a.dtype.itemsize for a in a_list)
    w_bytes = sum(w.size * (w.dtype.itemsize + (2 if emit_weights else 0)) for w in w_list)
    io_bytes = m * n * (jnp.dtype(out_dtype).itemsize + 4 * n_tiles)
    choice = None
    for tm in (2048, 1024, 512, 256, 128, 64, 32, 16, 8):
        for tn in (1024, 512, 256, 128):
            if m % tm or n % tn or (emit_weights and tm != m):
                continue
            if not all(w.allows(tn) for w in w_list if isinstance(w, _LayerWeight)):
                continue
            for a_buffers in (2, 1):
                est = _mm_vmem(tm, tn, a_list, w_list, len(dots), n_tiles, out_dtype, emit_weights, a_buffers)
                if est > MM_VMEM_BUDGET:
                    continue
                t_mxu = flops / MXU_FLOPS * (1.0 if tn % MXU_WIDTH == 0 and tm % MXU_WIDTH == 0 else 2.0)
                t_hbm = (a_bytes + (m // tm) * w_bytes + io_bytes) / HBM_BYTES_PER_S
                cost = max(t_mxu, t_hbm) + (m // tm) * (n // tn) * GRID_STEP_S
                if a_buffers == 1:
                    cost += a_bytes / HBM_BYTES_PER_S
                if choice is None or cost < choice[0]:
                    choice = (cost, tm, tn, est, a_buffers)
                break
    if choice is None:
        raise ValueError(f"{name}: no matmul tiling fits VMEM")
    _, tm, tn, est, a_buffers = choice
    a_mode = {} if a_buffers == 2 else {"pipeline_mode": pl.Buffered(1)}
    in_specs = [pl.BlockSpec((tm, a.shape[1]), lambda i, j: (i, 0), **a_mode) for a in a_list]
    in_specs += [w.block_spec(tn) if isinstance(w, _LayerWeight) else pl.BlockSpec((w.shape[0], tn), lambda i, j: (0, j))
                 for w in w_list]
    for e in extras:
        if e.shape[0] == 1:
            in_specs.append(pl.BlockSpec((1, tn), lambda i, j: (0, j)))
        else:
            in_specs.append(pl.BlockSpec((tm, tn), lambda i, j: (i, j)))
    body = functools.partial(_mm_body, n_a=len(a_list), n_w=len(w_list), n_e=len(extras), dots=tuple(dots),
                             epilogue=epilogue)
    out_shape = [jax.ShapeDtypeStruct((m, n), out_dtype)]
    out_specs = [pl.BlockSpec((tm, tn), lambda i, j: (i, j))]
    if emit_weights:
        assert m == tm
        out_shape += [jax.ShapeDtypeStruct(w.shape, BF16) for w in w_list]
        out_specs += [pl.BlockSpec((w.shape[0], tn), lambda i, j: (0, j)) for w in w_list]
    outs = pl.pallas_call(
        body,
        out_shape=out_shape,
        grid=(m // tm, n // tn),
        in_specs=in_specs,
        out_specs=out_specs,
        compiler_params=_params(("parallel", "parallel"), est),
        name=name,
    )(*a_list, *[w.stack if isinstance(w, _LayerWeight) else w for w in w_list], *extras)
    return (outs[0], outs[1:]) if emit_weights else outs[0]


def _ep_plain(accs, ex):
    return accs[0]


def _ep_swiglu(accs, ex):
    return _silu(accs[0]) * accs[1]


def _ep_half_residual(accs, ex):
    return ex[0] + 0.5 * accs[0]


def _ep_residual(accs, ex):
    return ex[0] + accs[0]


def _ep_residual2(accs, ex):
    return ex[0] + (accs[0] + accs[1])


def _ep_glu(accs, ex):
    return ex[0] * jax.nn.sigmoid(accs[0] + ex[1])


def _ep_gated_embed(accs, ex):
    return ex[0] + jax.nn.sigmoid(accs[0]) * accs[1]


def _ssd_body(z_ref, xs_ref, bm_ref, cm_ref, dtc_ref, dtp_ref, hx_ref, hb_ref, hc_ref, h0_ref,
              wx_ref, wb_ref, wc_ref, bx_ref, bb_ref, bc_ref, alc_ref, alp_ref, dbc_ref, dbp_ref, drep_ref, ng_ref,
              ya_ref, hfin_ref, state_sc, xp_sc, *, heads, head_dim, n_state, chunk, conv_k):
    c = pl.program_id(2)
    q, p, n = chunk, head_dim, n_state
    w = heads * p
    t0 = SUBLANES - (conv_k - 1)

    @pl.when(c == 0)
    def _load_carried():
        state_sc[...] = h0_ref[...]
        xp_sc[t0:SUBLANES, 0:w] = hx_ref[...]
        xp_sc[t0:SUBLANES, w:w + n] = hb_ref[...]
        xp_sc[t0:SUBLANES, w + n:w + 2 * n] = hc_ref[...]

    xp_sc[SUBLANES:SUBLANES + q, 0:w] = xs_ref[...]
    xp_sc[SUBLANES:SUBLANES + q, w:w + n] = bm_ref[...]
    xp_sc[SUBLANES:SUBLANES + q, w + n:w + 2 * n] = cm_ref[...]

    def conv_silu(lo, hi, w_ref, b_ref):
        acc = xp_sc[t0:t0 + q, lo:hi] * w_ref[0:1, :]
        for j in range(1, conv_k):
            acc = acc + xp_sc[t0 + j:t0 + j + q, lo:hi] * w_ref[j:j + 1, :]
        return _silu(acc + b_ref[...])

    xs = conv_silu(0, w, wx_ref, bx_ref)
    bm = conv_silu(w, w + n, wb_ref, bb_ref)
    cm = conv_silu(w + n, w + 2 * n, wc_ref, bc_ref)
    xp_sc[t0:SUBLANES, :] = xp_sc[SUBLANES + q - (conv_k - 1):SUBLANES + q, :]

    dt_c = jax.nn.softplus(dtc_ref[...] + dbc_ref[...])
    dt_p = jax.nn.softplus(dtp_ref[...] + dbp_ref[...])
    ii = lax.broadcasted_iota(jnp.int32, (q, q), 0)
    jj = lax.broadcasted_iota(jnp.int32, (q, q), 1)
    acum_c = jnp.dot((ii >= jj).astype(F32), dt_c * -jnp.exp(alc_ref[...]), precision=HIGHEST,
                     preferred_element_type=F32)
    i2 = lax.broadcasted_iota(jnp.int32, (2 * q, 2 * q), 0)
    j2 = lax.broadcasted_iota(jnp.int32, (2 * q, 2 * q), 1)
    same_head = (i2 >= q) == (j2 >= q)
    acum_p = jnp.dot(dt_p * -jnp.exp(alp_ref[...]), ((i2 <= j2) & same_head).astype(F32), precision=HIGHEST,
                     preferred_element_type=F32)

    lane = lax.broadcasted_iota(jnp.int32, (q, LANES), 1)
    first_head = lane < p

    def per_head_lanes(v):
        return jnp.concatenate([jnp.where(first_head, v[:, 2 * k:2 * k + 1], v[:, 2 * k + 1:2 * k + 2])
                                for k in range(heads // 2)], axis=1)

    acum_w = per_head_lanes(acum_c)
    grow_w = jnp.exp(acum_w)
    wend_w = jnp.exp(acum_w[q - 1:q, :] - acum_w) * per_head_lanes(dt_c)

    cm_b = cm.astype(BF16)
    bm_b = bm.astype(BF16)
    scores2 = lax.dot_general(cm_b, jnp.concatenate([bm_b, bm_b], axis=0), NT_DIMS,
                              preferred_element_type=F32)
    row = lax.broadcasted_iota(jnp.int32, (q, 2 * q), 0)
    col2 = lax.broadcasted_iota(jnp.int32, (q, 2 * q), 1)
    causal2 = row >= jnp.where(col2 >= q, col2 - q, col2)
    y_pairs = []
    for k in range(heads // 2):
        seg = acum_w[:, k * LANES:(k + 1) * LANES] - acum_p[k:k + 1, :]
        decay = jnp.exp(jnp.where(causal2, seg, -jnp.inf))
        m2 = (scores2 * decay * dt_p[k:k + 1, :]).astype(BF16)
        x_pair = xs[:, k * LANES:(k + 1) * LANES]
        x2 = jnp.concatenate([jnp.where(first_head, x_pair, 0.0), jnp.where(first_head, 0.0, x_pair)], axis=0)
        y_pairs.append(jnp.dot(m2, x2.astype(BF16), preferred_element_type=F32))
    st = state_sc[...]
    y_off = jnp.dot(cm_b, st.astype(BF16), preferred_element_type=F32)
    xw = (xs * wend_w).astype(BF16)
    state_sc[...] = st * grow_w[q - 1:q, :] + lax.dot_general(bm_b, xw, TN_DIMS, preferred_element_type=F32)

    y = jnp.concatenate(y_pairs, axis=1) + y_off * grow_w + drep_ref[...] * xs
    y = y * _silu(z_ref[...])
    inv = lax.rsqrt(jnp.mean(y * y, axis=-1, keepdims=True) + EPS)
    ya_ref[...] = (y * inv * ng_ref[...]).astype(ya_ref.dtype)

    @pl.when(c == pl.num_programs(2) - 1)
    def _store_state():
        hfin_ref[...] = state_sc[...]


def _ssd_mixer(zx, dt_raw, conv_hist, h0, conv_w, conv_b, a_log, dt_bias, d_skip, norm_g):
    b, l, _ = zx.shape
    _, heads, p, n = h0.shape
    inner = heads * p
    conv_k, conv_dim = conv_w.shape
    g = (conv_dim - inner) // (2 * n)
    r = heads // g
    w = r * p
    q = SSD_CHUNK if l % SSD_CHUNK == 0 else l
    nc = l // q
    assert 2 * p == LANES and r % 2 == 0 and 2 * q == LANES, "two heads / two chunk copies per 128-lane tile"
    dt5 = dt_raw.reshape(b, nc, q, g, r)
    dtc = dt5.transpose(0, 3, 1, 2, 4)
    dtp = dt5.transpose(0, 3, 1, 4, 2).reshape(b, g, nc, r // 2, 2 * q)
    pair_lanes = lambda v: jnp.repeat(v.reshape(g, r // 2, 2), q, axis=-1)
    state0 = h0.reshape(b, g, r, p, n).transpose(0, 1, 4, 2, 3).reshape(b, g, n, w)
    conv_b2 = conv_b.reshape(1, conv_dim)
    nb0 = inner // n

    def col(block_w, first):
        return pl.BlockSpec((None, q, block_w), lambda bi, gi, ci: (bi, ci, first + gi))

    def hist(block_w, first):
        return pl.BlockSpec((None, conv_k - 1, block_w), lambda bi, gi, ci: (bi, 0, first + gi))

    def wrow(rows, block_w, first):
        return pl.BlockSpec((rows, block_w), lambda bi, gi, ci: (0, first + gi))

    def per_group(shape):
        return pl.BlockSpec((None,) + shape, lambda bi, gi, ci: (gi, 0, 0))

    in_specs = [
        col(w, 0), col(w, inner // w), col(n, 2 * inner // n), col(n, 2 * inner // n + g),
        pl.BlockSpec((None, None, None, q, r), lambda bi, gi, ci: (bi, gi, ci, 0, 0)),
        pl.BlockSpec((None, None, None, r // 2, 2 * q), lambda bi, gi, ci: (bi, gi, ci, 0, 0)),
        hist(w, 0), hist(n, nb0), hist(n, nb0 + g),
        pl.BlockSpec((None, None, n, w), lambda bi, gi, ci: (bi, gi, 0, 0)),
        wrow(conv_k, w, 0), wrow(conv_k, n, nb0), wrow(conv_k, n, nb0 + g),
        wrow(1, w, 0), wrow(1, n, nb0), wrow(1, n, nb0 + g),
        per_group((1, r)), per_group((r // 2, 2 * q)), per_group((1, r)), per_group((r // 2, 2 * q)),
        per_group((1, w)), per_group((1, w)),
    ]
    out_specs = [
        pl.BlockSpec((None, q, w), lambda bi, gi, ci: (bi, ci, gi)),
        pl.BlockSpec((None, None, n, w), lambda bi, gi, ci: (bi, gi, 0, 0)),
    ]
    est = 4 * (2 * q * (2 * w + 2 * n) + 6 * n * w + (SUBLANES + q) * (w + 2 * n) + 12 * q * w)
    body = functools.partial(_ssd_body, heads=r, head_dim=p, n_state=n, chunk=q, conv_k=conv_k)
    ya, state = pl.pallas_call(
        body,
        out_shape=[jax.ShapeDtypeStruct((b, l, inner), BF16), jax.ShapeDtypeStruct((b, g, n, w), F32)],
        grid=(b, g, nc),
        in_specs=in_specs,
        out_specs=out_specs,
        scratch_shapes=[pltpu.VMEM((n, w), F32), pltpu.VMEM((SUBLANES + q, w + 2 * n), F32)],
        compiler_params=_params(("parallel", "parallel", "arbitrary"), est),
        name="ssd_mixer",
    )(zx, zx, zx, zx, dtc, dtp, conv_hist, conv_hist, conv_hist, state0,
      conv_w, conv_w, conv_w, conv_b2, conv_b2, conv_b2,
      a_log.reshape(g, 1, r), pair_lanes(a_log), dt_bias.reshape(g, 1, r), pair_lanes(dt_bias),
      jnp.repeat(d_skip, p).reshape(g, 1, w), norm_g.reshape(g, 1, w))
    h_fin = state.reshape(b, g, n, r, p).transpose(0, 1, 3, 4, 2).reshape(b, heads, p, n)
    return ya, h_fin


def _gelu_tanh(x):
    return 0.5 * x * (1.0 + jnp.tanh(0.7978845608028654 * (x + 0.044715 * (x * x * x))))


def _s5_body(u_ref, h0re_ref, h0im_ref, lam_ref, bblk_ref, cblk_ref, d_ref,
             y_ref, ybf_ref, hre_ref, him_ref, bu_sc, h_sc, *, n_seq, n_pack, t_blk, pitch, n_tiles, unroll):
    j = pl.program_id(1)
    half = n_tiles // 2
    rows = n_seq * t_blk
    cw = u_ref.shape[-1] // n_pack
    sw = half * LANES

    @pl.when(j == 0)
    def _load_state():
        for pk in range(n_pack):
            for k in range(half):
                cols = slice(pk * sw + k * LANES, pk * sw + (k + 1) * LANES)
                h_sc[k, pk * n_seq:(pk + 1) * n_seq, :] = h0re_ref[:, cols]
                h_sc[half + k, pk * n_seq:(pk + 1) * n_seq, :] = h0im_ref[:, cols]

    for pk in range(n_pack):
        u_b = u_ref[:, :, pk * cw:(pk + 1) * cw].reshape(rows, cw).astype(BF16)
        for kk in range(half):
            bu = jnp.dot(u_b, bblk_ref[pk, :, kk * 2 * LANES:(kk + 1) * 2 * LANES], preferred_element_type=F32)
            for s in range(n_seq):
                base = (pk * n_seq + s) * pitch
                bu_sc[2 * kk, base:base + t_blk, :] = bu[s * t_blk:(s + 1) * t_blk, :LANES]
                bu_sc[2 * kk + 1, base:base + t_blk, :] = bu[s * t_blk:(s + 1) * t_blk, LANES:]

    def lam_rows(k):
        return jnp.concatenate([jnp.broadcast_to(lam_ref[pk, :, k * LANES:(k + 1) * LANES], (n_seq, LANES))
                                for pk in range(n_pack)], axis=0)

    lam_re = [lam_rows(k) for k in range(half)]
    lam_im = [lam_rows(half + k) for k in range(half)]

    def step(t, hs):
        idx = pl.ds(t, n_pack * n_seq, stride=pitch)
        new_re, new_im = [], []
        for k in range(half):
            h_re, h_im = hs[k], hs[half + k]
            re_rows, im_rows = bu_sc.at[k], bu_sc.at[half + k]
            n_re = lam_re[k] * h_re - lam_im[k] * h_im + re_rows[idx, :]
            n_im = lam_re[k] * h_im + lam_im[k] * h_re + im_rows[idx, :]
            re_rows[idx, :] = n_re
            im_rows[idx, :] = n_im
            new_re.append(n_re)
            new_im.append(n_im)
        return tuple(new_re + new_im)

    hs = lax.fori_loop(0, t_blk, step, tuple(h_sc[k] for k in range(n_tiles)), unroll=unroll)
    for k in range(n_tiles):
        h_sc[k] = hs[k]

    for pk in range(n_pack):
        h_all = jnp.concatenate(
            [jnp.concatenate([bu_sc[k, (pk * n_seq + s) * pitch:(pk * n_seq + s) * pitch + t_blk, :].astype(BF16)
                              for s in range(n_seq)], axis=0) for k in range(n_tiles)], axis=1)
        u = u_ref[:, :, pk * cw:(pk + 1) * cw].reshape(rows, cw)
        y = jnp.dot(h_all, cblk_ref[pk], preferred_element_type=F32) + d_ref[:, pk * cw:(pk + 1) * cw] * u
        y = _gelu_tanh(y)
        y_ref[:, :, pk * cw:(pk + 1) * cw] = y.reshape(n_seq, t_blk, cw)
        ybf_ref[:, :, pk * cw:(pk + 1) * cw] = y.astype(BF16).reshape(n_seq, t_blk, cw)

    @pl.when(j == pl.num_programs(1) - 1)
    def _store_state():
        for pk in range(n_pack):
            for k in range(half):
                cols = slice(pk * sw + k * LANES, pk * sw + (k + 1) * LANES)
                hre_ref[:, cols] = hs[k][pk * n_seq:(pk + 1) * n_seq, :]
                him_ref[:, cols] = hs[half + k][pk * n_seq:(pk + 1) * n_seq, :]


def _s5_discretize(lam_re, lam_im, log_step, b_re, b_im, c_re, c_im):
    g, p, k = b_re.shape
    gb = S5_GROUP_BLOCK
    nb = g // gb
    step = jnp.exp(log_step)[:, None]
    mag = jnp.exp(lam_re * step)
    lb_re, lb_im = mag * jnp.cos(lam_im * step), mag * jnp.sin(lam_im * step)
    den = lam_re * lam_re + lam_im * lam_im
    f_re = ((lb_re - 1.0) * lam_re + lb_im * lam_im) / den
    f_im = (lb_im * lam_re - (lb_re - 1.0) * lam_im) / den
    bb_re = f_re[..., None] * b_re - f_im[..., None] * b_im
    bb_im = f_re[..., None] * b_im + f_im[..., None] * b_re
    eye = jnp.eye(gb, dtype=F32)
    lam = jnp.stack([lb_re.reshape(nb, gb * p), lb_im.reshape(nb, gb * p)], axis=1).reshape(nb, 1, 2 * gb * p)
    bb = jnp.stack([bb_re, bb_im]).reshape(2, nb, gb, p, k).transpose(1, 2, 4, 0, 3)
    bblk = (bb[:, :, :, :, None, :] * eye[None, :, None, None, :, None]).reshape(nb, gb * k, 2 * gb * p)
    cc = jnp.stack([c_re, -c_im]).reshape(2, nb, gb, k, p).transpose(1, 0, 2, 4, 3)
    cblk = (cc[:, :, :, :, None, :] * eye[None, None, :, None, :, None]).reshape(nb, 2 * gb * p, gb * k)
    return lam, bblk.astype(BF16), cblk.astype(BF16)


def _s5_mixer(u, h0_re, h0_im, lam, bblk, cblk, d_skip):
    b, l, width = u.shape
    _, g, p = h0_re.shape
    nb, cw, sw2 = bblk.shape
    n_tiles = sw2 // LANES
    n_pack = max(1, SUBLANES // b)
    if nb % n_pack:
        n_pack = 1
    t_blk = _largest_divisor(l, (256, 128, 64, 32, 16, 8))
    rows = n_pack * b * t_blk
    pitch = SUBLANES * (t_blk // SUBLANES + (1 - (t_blk // SUBLANES) % 2))
    est = 6 * rows * sw2 + 2 * rows * cw * 10 + 8 * n_pack * cw * sw2 + 6 * rows * cw * 4
    body = functools.partial(_s5_body, n_seq=b, n_pack=n_pack, t_blk=t_blk, pitch=pitch, n_tiles=n_tiles,
                             unroll=S5_SCAN_UNROLL)
    seq_spec = pl.BlockSpec((b, t_blk, n_pack * cw), lambda gi, ti: (0, ti, gi))
    state_spec = pl.BlockSpec((b, n_pack * sw2 // 2), lambda gi, ti: (0, gi))
    y, ybf, h_re, h_im = pl.pallas_call(
        body,
        out_shape=[jax.ShapeDtypeStruct((b, l, width), F32), jax.ShapeDtypeStruct((b, l, width), BF16),
                   jax.ShapeDtypeStruct((b, g * p), F32), jax.ShapeDtypeStruct((b, g * p), F32)],
        grid=(nb // n_pack, l // t_blk),
        in_specs=[seq_spec, state_spec, state_spec,
                  pl.BlockSpec((n_pack, 1, sw2), lambda gi, ti: (gi, 0, 0)),
                  pl.BlockSpec((n_pack, cw, sw2), lambda gi, ti: (gi, 0, 0)),
                  pl.BlockSpec((n_pack, sw2, cw), lambda gi, ti: (gi, 0, 0)),
                  pl.BlockSpec((1, n_pack * cw), lambda gi, ti: (0, gi))],
        out_specs=[seq_spec, seq_spec, state_spec, state_spec],
        scratch_shapes=[pltpu.VMEM((n_tiles, n_pack * b * pitch, LANES), F32),
                        pltpu.VMEM((n_tiles, n_pack * b, LANES), F32)],
        compiler_params=_params(("parallel", "arbitrary"), est),
        name="s5_mixer",
    )(u, h0_re.reshape(b, g * p), h0_im.reshape(b, g * p), lam, bblk, cblk, d_skip.reshape(1, width))
    return y, ybf, h_re.reshape(b, g, p), h_im.reshape(b, g, p)


def _short_conv_body(b_ref, c_ref, v_ref, hist_ref, w_ref, o_ref, hnew_ref, cv_sc, *, t_blk, conv_k):
    t = pl.program_id(2)
    t0 = SUBLANES - (conv_k - 1)

    @pl.when(t == 0)
    def _load_hist():
        cv_sc[t0:SUBLANES, :] = hist_ref[...]

    cv_sc[SUBLANES:SUBLANES + t_blk, :] = c_ref[...] * v_ref[...]
    acc = cv_sc[t0:t0 + t_blk, :] * w_ref[0:1, :]
    for j in range(1, conv_k):
        acc = acc + cv_sc[t0 + j:t0 + j + t_blk, :] * w_ref[j:j + 1, :]
    o_ref[...] = (b_ref[...] * acc).astype(o_ref.dtype)
    cv_sc[t0:SUBLANES, :] = cv_sc[SUBLANES + t_blk - (conv_k - 1):SUBLANES + t_blk, :]

    @pl.when(t == pl.num_programs(2) - 1)
    def _store_hist():
        hnew_ref[...] = cv_sc[t0:SUBLANES, :]


def _short_conv(bcv, hist, conv_w):
    bsz, l, width3 = bcv.shape
    width = width3 // 3
    conv_k = conv_w.shape[0]
    t_blk = _largest_divisor(l, (256, 128, 64, 32, 16, 8))
    tw = _largest_divisor(width, (1024, 512, 256, 128))
    nw = width // tw

    def col(first):
        return pl.BlockSpec((None, t_blk, tw), lambda bi, wi, ti: (bi, ti, first + wi))

    body = functools.partial(_short_conv_body, t_blk=t_blk, conv_k=conv_k)
    return pl.pallas_call(
        body,
        out_shape=[jax.ShapeDtypeStruct((bsz, l, width), BF16), jax.ShapeDtypeStruct(hist.shape, F32)],
        grid=(bsz, nw, l // t_blk),
        in_specs=[col(0), col(nw), col(2 * nw),
                  pl.BlockSpec((None, conv_k - 1, tw), lambda bi, wi, ti: (bi, 0, wi)),
                  pl.BlockSpec((conv_k, tw), lambda bi, wi, ti: (0, wi))],
        out_specs=[pl.BlockSpec((None, t_blk, tw), lambda bi, wi, ti: (bi, ti, wi)),
                   pl.BlockSpec((None, conv_k - 1, tw), lambda bi, wi, ti: (bi, 0, wi))],
        scratch_shapes=[pltpu.VMEM((SUBLANES + t_blk, tw), F32)],
        compiler_params=_params(("parallel", "parallel", "arbitrary"), 4 * t_blk * tw * 12),
        name="short_conv",
    )(bcv, bcv, bcv, hist, conv_w)


def _prepare_weights(w):
    heads, p, _ = w["ssd_shape"]
    inner = heads * p
    conv_dim = w["ssd_conv_w"].shape[-1]
    out = {}
    assert w["w_out0"].shape[1] == 2 * inner
    out["w_u"] = w["w_in0"][:, :, inner + conv_dim + heads:].astype(BF16)
    out["s5"] = [_s5_discretize(w["s5_lam_re"][j], w["s5_lam_im"][j], w["s5_log_step"][j], w["s5_b_re"][j],
                                w["s5_b_im"][j], w["s5_c_re"][j], w["s5_c_im"][j])
                 for j in range(w["s5_lam_re"].shape[0])]
    return out


def _run_trunk(x3, p4, conv_ssd, h_ssd, s5_re, s5_im, conv_sc, w, wb, emitted):
    bsz, l, d = x3.shape
    m = bsz * l
    depth = w["g_ffn1"].shape[0]
    heads = h_ssd.shape[2]
    x = x3.reshape(m, d)
    inner = heads * h_ssd.shape[3]
    zx_cols = inner + w["ssd_conv_w"].shape[-1]

    def f32_window(name, i):
        if name == "w_zx":
            return _LayerWeight(w["w_in0"], i, n=zx_cols)
        if name == "w_dt":
            return _LayerWeight(w["w_in0"], i, col_start=zx_cols, n=_round_up(heads, LANES))
        if name in ("w_out0_a", "w_out0_b"):
            return _LayerWeight(w["w_out0"], i, k=inner, row_block=0 if name == "w_out0_a" else 1)
        return _LayerWeight(w[name], i)

    def mm(a_list, names, i, dots, extras, epilogue, out_dtype, tag):
        if names[0] in wb:
            ws = [wb[nm][i] if isinstance(wb[nm], dict) else _LayerWeight(wb[nm], i) for nm in names]
            return _fused_matmul(a_list, ws, dots, extras, epilogue, out_dtype, tag)
        out, copies = _fused_matmul(a_list, [f32_window(nm, i) for nm in names], dots, extras, epilogue,
                                    out_dtype, tag, emit_weights=True)
        for nm, copy in zip(names, copies):
            emitted.setdefault(nm, {})[i] = copy
        return out

    def ffn_half_step(x, g, i, tag):
        xn = _rmsnorm(x, g, BF16)
        h = mm([xn], [f"w_{tag}_gate", f"w_{tag}_up"], i, [(0, 0), (0, 1)], [], _ep_swiglu, BF16, f"{tag}_gate_up")
        return mm([h], [f"w_{tag}_down"], i, [(0, 0)], [x], _ep_half_residual, F32, f"{tag}_down")

    out_conv_ssd, out_h, out_re, out_im, out_sc = [], [], [], [], []
    for i in range(depth):
        x = ffn_half_step(x, w["g_ffn1"][i], i, "ffn1")
        xn = _rmsnorm(x, w["g_mix"][i], BF16)
        j = i // 2
        if i % 2 == 0:
            zx = mm([xn], ["w_zx"], j, [(0, 0)], [], _ep_plain, F32, "in0_zxbc")
            dt = mm([xn], ["w_dt"], j, [(0, 0)], [], _ep_plain, F32, "in0_dt")[:, :heads]
            u = mm([xn], ["w_u"], j, [(0, 0)], [], _ep_plain, F32, "in0_u")
            zx3 = zx.reshape(bsz, l, zx.shape[-1])
            ya, h_new = _ssd_mixer(zx3, dt.reshape(bsz, l, heads), conv_ssd[j], h_ssd[j], w["ssd_conv_w"][j],
                                   w["ssd_conv_b"][j], w["ssd_a_log"][j], w["ssd_dt_bias"][j], w["ssd_d"][j],
                                   w["ssd_norm_g"][j])
            inner = ya.shape[-1]
            conv_k, conv_dim = w["ssd_conv_w"][j].shape
            xbc_tail = zx3[:, l - min(l, conv_k - 1):, inner:inner + conv_dim]
            c_new = jnp.concatenate([conv_ssd[j], xbc_tail], axis=1)[:, -(conv_k - 1):]
            lam, bblk, cblk = wb["s5"][j]
            yb, yb_bf, re_new, im_new = _s5_mixer(u.reshape(bsz, l, u.shape[-1]), s5_re[j], s5_im[j], lam, bblk, cblk,
                                                  w["s5_d"][j])
            width = yb.shape[-1]
            yb_glu = mm([yb_bf.reshape(m, width)], ["s5_w_glu"], j, [(0, 0)],
                        [yb.reshape(m, width), w["s5_b_glu"][j].reshape(1, width)], _ep_glu, BF16, "s5_glu")
            x = mm([ya.reshape(m, inner), yb_glu], ["w_out0_a", "w_out0_b"], j, [(0, 0), (1, 1)], [x],
                   _ep_residual2, F32, "out0")
            out_conv_ssd.append(c_new)
            out_h.append(h_new)
            out_re.append(re_new)
            out_im.append(im_new)
        else:
            bcv = mm([xn], ["w_in1"], j, [(0, 0)], [], _ep_plain, F32, "in1")
            yc, sc_new = _short_conv(bcv.reshape(bsz, l, bcv.shape[-1]), conv_sc[j], w["sc_conv_w"][j])
            x = mm([yc.reshape(m, yc.shape[-1])], ["w_out1"], j, [(0, 0)], [x], _ep_residual, F32, "out1")
            out_sc.append(sc_new)
        x = ffn_half_step(x, w["g_ffn2"][i], i, "ffn2")
        xn = _rmsnorm(x, w["g_pe"][i], BF16)
        x = mm([xn, p4[i].reshape(m, p4.shape[-1])], ["w_pe_gate", "w_pe_proj"], i, [(0, 0), (1, 1)], [x],
               _ep_gated_embed, F32, "pe_embed")
    y = _rmsnorm(x, w["g_final"], F32).reshape(bsz, l, d)
    return (y, jnp.stack(out_conv_ssd), jnp.stack(out_h), jnp.stack(out_re), jnp.stack(out_im), jnp.stack(out_sc))


def kernel(x_prompt, x_sample, p_prompt, p_sample, cache_conv_ssd, state_ssd, state_s5_re, state_s5_im, cache_conv_short, g_ffn1, w_ffn1_gate, w_ffn1_up, w_ffn1_down, g_mix, g_ffn2, w_ffn2_gate, w_ffn2_up, w_ffn2_down, g_pe, w_pe_gate, w_pe_proj, w_in0, ssd_conv_w, ssd_conv_b, ssd_a_log, ssd_dt_bias, ssd_d, ssd_norm_g, s5_lam_re, s5_lam_im, s5_log_step, s5_b_re, s5_b_im, s5_c_re, s5_c_im, s5_d, s5_w_glu, s5_b_glu, w_out0, w_in1, sc_conv_w, w_out1, g_final):
    w = dict(g_ffn1=g_ffn1, w_ffn1_gate=w_ffn1_gate, w_ffn1_up=w_ffn1_up, w_ffn1_down=w_ffn1_down,
             g_mix=g_mix, g_ffn2=g_ffn2, w_ffn2_gate=w_ffn2_gate, w_ffn2_up=w_ffn2_up, w_ffn2_down=w_ffn2_down,
             g_pe=g_pe, w_pe_gate=w_pe_gate, w_pe_proj=w_pe_proj, w_in0=w_in0, ssd_conv_w=ssd_conv_w,
             ssd_conv_b=ssd_conv_b, ssd_a_log=ssd_a_log, ssd_dt_bias=ssd_dt_bias, ssd_d=ssd_d,
             ssd_norm_g=ssd_norm_g, s5_lam_re=s5_lam_re, s5_lam_im=s5_lam_im, s5_log_step=s5_log_step,
             s5_b_re=s5_b_re, s5_b_im=s5_b_im, s5_c_re=s5_c_re, s5_c_im=s5_c_im, s5_d=s5_d,
             s5_w_glu=s5_w_glu, s5_b_glu=s5_b_glu, w_out0=w_out0, w_in1=w_in1, sc_conv_w=sc_conv_w,
             w_out1=w_out1, g_final=g_final, ssd_shape=state_ssd.shape[2:])
    wb = _prepare_weights(w)
    emitted = {}
    sample = _run_trunk(x_sample, p_sample, cache_conv_ssd, state_ssd, state_s5_re, state_s5_im, cache_conv_short, w,
                        wb, emitted)
    wb.update(emitted)
    b = x_prompt.shape[0]
    zeros = lambda ref: jnp.zeros((ref.shape[0], b) + ref.shape[2:], F32)
    prompt = _run_trunk(x_prompt, p_prompt, zeros(cache_conv_ssd), zeros(state_ssd), zeros(state_s5_re),
                        zeros(state_s5_im), zeros(cache_conv_short), w, wb, None)
    return (prompt[0], sample[0]) + prompt[1:] + sample[1:]
```

```python
import functools

import jax
import jax.numpy as jnp
from jax import lax
from jax.experimental import pallas as pl
from jax.experimental.pallas import tpu as pltpu

F32 = jnp.float32
BF16 = jnp.bfloat16
EPS = 1e-6
SSD_CHUNK = 64
SSD_GROUPS_PER_STEP = 4
S5_GROUP_BLOCK = 16
S5_SCAN_UNROLL = 8
CONV_MM_ROWS = 1024
PREV_ROWS = 16
LANES = 128
SUBLANES = 8
V7X_VMEM_BYTES = 64 * 1024 * 1024
MM_VMEM_BUDGET = 42 * 1024 * 1024
MXU_WIDTH = 256
MXU_FLOPS = 0.95e15
HBM_BYTES_PER_S = 3.0e12
GRID_STEP_S = 0.35e-6
HIGHEST = lax.Precision.HIGHEST
NT_DIMS = (((1,), (1,)), ((), ()))
TN_DIMS = (((0,), (0,)), ((), ()))


def _largest_divisor(n, candidates):
    for c in candidates:
        if n % c == 0:
            return c
    raise ValueError(f"no tile in {candidates} divides {n}")


def _round_up(n, m):
    return (n + m - 1) // m * m


def _params(semantics, vmem_estimate):
    limit = min(V7X_VMEM_BYTES - 8 * 1024 * 1024, int(vmem_estimate * 1.25) + 8 * 1024 * 1024)
    return pltpu.CompilerParams(dimension_semantics=semantics, vmem_limit_bytes=limit)


def _silu(x):
    return x * jax.nn.sigmoid(x)


def _rmsnorm_body(x_ref, g_ref, o_ref):
    x = x_ref[...]
    inv = lax.rsqrt(jnp.mean(x * x, axis=-1, keepdims=True) + EPS)
    o_ref[...] = (x * inv * g_ref[...]).astype(o_ref.dtype)


def _rmsnorm(x, g, out_dtype):
    m, d = x.shape
    tm = _largest_divisor(m, (256, 128, 64, 32, 16, 8))
    return pl.pallas_call(
        _rmsnorm_body,
        out_shape=jax.ShapeDtypeStruct((m, d), out_dtype),
        grid=(m // tm,),
        in_specs=[pl.BlockSpec((tm, d), lambda i: (i, 0)), pl.BlockSpec((1, d), lambda i: (0, 0))],
        out_specs=pl.BlockSpec((tm, d), lambda i: (i, 0)),
        compiler_params=_params(("parallel",), 2 * tm * d * (4 + jnp.dtype(out_dtype).itemsize) + 3 * tm * d * 4),
        name="rmsnorm",
    )(x, g.reshape(1, d))


def _mm_body(*refs, n_a, n_w, n_e, dots, epilogue):
    a_refs, w_refs, e_refs = refs[:n_a], refs[n_a:n_a + n_w], refs[n_a + n_w:n_a + n_w + n_e]
    o_ref, w_out_refs = refs[n_a + n_w + n_e], refs[n_a + n_w + n_e + 1:]
    w_vals = [w[...] if w.dtype == BF16 else w[...].astype(BF16) for w in w_refs]
    a_vals = {}
    accs = []
    for ai, wi in dots:
        if ai not in a_vals:
            a_vals[ai] = a_refs[ai][...].astype(BF16)
        accs.append(jnp.dot(a_vals[ai], w_vals[wi], preferred_element_type=F32))
    o_ref[...] = epilogue(accs, [e[...] for e in e_refs]).astype(o_ref.dtype)
    for w_out, w_val in zip(w_out_refs, w_vals):
        w_out[...] = w_val


class _LayerWeight:
    def __init__(self, stack, layer, k=None, row_block=0, col_start=0, n=None):
        self.stack, self.layer, self.row_block, self.col_start = stack, layer, row_block, col_start
        self.shape = (k or stack.shape[1], n or stack.shape[2])
        self.dtype = stack.dtype
        self.size = self.shape[0] * self.shape[1]

    def allows(self, tn):
        return self.col_start % tn == 0

    def block_spec(self, tn):
        first = self.col_start // tn
        return pl.BlockSpec((None, self.shape[0], tn), lambda i, j: (self.layer, self.row_block, first + j))


def _mm_vmem(tm, tn, a_list, w_list, n_dots, tiles, out_dtype, emit_weights):
    a_bytes = sum(tm * a.shape[1] * a.dtype.itemsize for a in a_list)
    w_bytes = sum(w.shape[0] * tn * (w.dtype.itemsize + (2 if emit_weights else 0)) for w in w_list)
    io_bytes = tm * tn * (jnp.dtype(out_dtype).itemsize + 4 * tiles)
    return 2 * (a_bytes + w_bytes + io_bytes) + (n_dots + 1) * tm * tn * 4


def _fused_matmul(a_list, w_list, dots, extras, epilogue, out_dtype, name, emit_weights=False):
    m, n = a_list[0].shape[0], w_list[0].shape[1]
    n_tiles = sum(1 for e in extras if e.shape[0] != 1)
    flops = 2.0 * m * n * sum(w_list[wi].shape[0] for _, wi in dots)
    a_bytes = sum(a.size * a.dtype.itemsize for a in a_list)
    w_bytes = sum(w.size * (w.dtype.itemsize + (2 if emit_weights else 0)) for w in w_list)
    io_bytes = m * n * (jnp.dtype(out_dtype).itemsize + 4 * n_tiles)
    choice = None
    for tm in (1024, 512, 256, 128, 64, 32, 16, 8):
        for tn in (1024, 512, 256, 128):
            if m % tm or n % tn or (emit_weights and tm != m):
                continue
            if not all(w.allows(tn) for w in w_list if isinstance(w, _LayerWeight)):
                continue
            est = _mm_vmem(tm, tn, a_list, w_list, len(dots), n_tiles, out_dtype, emit_weights)
            if est > MM_VMEM_BUDGET:
                continue
            t_mxu = flops / MXU_FLOPS * (1.0 if tn % MXU_WIDTH == 0 and tm % MXU_WIDTH == 0 else 2.0)
            t_hbm = (a_bytes + (m // tm) * w_bytes + io_bytes) / HBM_BYTES_PER_S
            cost = max(t_mxu, t_hbm) + (m // tm) * (n // tn) * GRID_STEP_S
            if choice is None or cost < choice[0]:
                choice = (cost, tm, tn, est)
    if choice is None:
        raise ValueError(f"{name}: no matmul tiling fits VMEM")
    _, tm, tn, est = choice
    in_specs = [pl.BlockSpec((tm, a.shape[1]), lambda i, j: (i, 0)) for a in a_list]
    in_specs += [w.block_spec(tn) if isinstance(w, _LayerWeight) else pl.BlockSpec((w.shape[0], tn), lambda i, j: (0, j))
                 for w in w_list]
    for e in extras:
        if e.shape[0] == 1:
            in_specs.append(pl.BlockSpec((1, tn), lambda i, j: (0, j)))
        else:
            in_specs.append(pl.BlockSpec((tm, tn), lambda i, j: (i, j)))
    body = functools.partial(_mm_body, n_a=len(a_list), n_w=len(w_list), n_e=len(extras), dots=tuple(dots),
                             epilogue=epilogue)
    out_shape = [jax.ShapeDtypeStruct((m, n), out_dtype)]
    out_specs = [pl.BlockSpec((tm, tn), lambda i, j: (i, j))]
    if emit_weights:
        assert m == tm
        out_shape += [jax.ShapeDtypeStruct(w.shape, BF16) for w in w_list]
        out_specs += [pl.BlockSpec((w.shape[0], tn), lambda i, j: (0, j)) for w in w_list]
    outs = pl.pallas_call(
        body,
        out_shape=out_shape,
        grid=(m // tm, n // tn),
        in_specs=in_specs,
        out_specs=out_specs,
        compiler_params=_params(("parallel", "parallel"), est),
        name=name,
    )(*a_list, *[w.stack if isinstance(w, _LayerWeight) else w for w in w_list], *extras)
    return (outs[0], outs[1:]) if emit_weights else outs[0]


def _ep_plain(accs, ex):
    return accs[0]


def _ep_swiglu(accs, ex):
    return _silu(accs[0]) * accs[1]


def _ep_half_residual(accs, ex):
    return ex[0] + 0.5 * accs[0]


def _ep_residual(accs, ex):
    return ex[0] + accs[0]


def _ep_residual2(accs, ex):
    return ex[0] + (accs[0] + accs[1])


def _ep_glu(accs, ex):
    return ex[0] * jax.nn.sigmoid(accs[0] + ex[1])


def _ep_gated_embed(accs, ex):
    return ex[0] + jax.nn.sigmoid(accs[0]) * accs[1]


def _causal_taps(sc, cur, prev, taps_ref, conv_k):
    tm = cur.shape[0]
    t0 = SUBLANES - (conv_k - 1)
    sc[t0:SUBLANES, :] = prev
    sc[SUBLANES:SUBLANES + tm, :] = cur
    out = cur * taps_ref[conv_k - 1:conv_k, :]
    for j in range(conv_k - 1):
        out = out + sc[t0 + j:t0 + j + tm, :] * taps_ref[j:j + 1, :]
    return out


def _in1_conv_body(a_ref, aprev_ref, wb_ref, wc_ref, wv_ref, hist_ref, taps_ref, o_ref, tail_ref, sc, *,
                   conv_k, blocks_per_seq):
    i = pl.program_id(0)
    tm = a_ref.shape[0]
    a, ap = a_ref[...], aprev_ref[...]
    wc, wv = wc_ref[...], wv_ref[...]
    cv = jnp.dot(a, wc, preferred_element_type=F32) * jnp.dot(a, wv, preferred_element_type=F32)
    tail_ref[...] = cv[tm - SUBLANES:, :]
    prev = jnp.dot(ap, wc, preferred_element_type=F32) * jnp.dot(ap, wv, preferred_element_type=F32)
    prev = jnp.where(i % blocks_per_seq == 0, hist_ref[...], prev[PREV_ROWS - (conv_k - 1):, :])
    y = _causal_taps(sc, cv, prev, taps_ref, conv_k)
    o_ref[...] = (jnp.dot(a, wb_ref[...], preferred_element_type=F32) * y).astype(o_ref.dtype)


def _short_conv_projection(xn, w2d, hist, taps, seq_len):
    m, k = xn.shape
    n = w2d.shape[1] // 3
    tm = CONV_MM_ROWS
    tn = _largest_divisor(n, (256, 128))
    nw = n // tn
    blocks_per_seq = seq_len // tm
    conv_k = taps.shape[0]
    seq_of = lambda i: i // blocks_per_seq
    est = 2 * (tm * k * 2 + 3 * k * tn * 2 + tm * tn * 4) + (SUBLANES + tm) * tn * 4 + 4 * tm * tn * 4
    return pl.pallas_call(
        functools.partial(_in1_conv_body, conv_k=conv_k, blocks_per_seq=blocks_per_seq),
        out_shape=[jax.ShapeDtypeStruct((m, n), BF16), jax.ShapeDtypeStruct((m // seq_len, SUBLANES, n), F32)],
        grid=(m // tm, n // tn),
        in_specs=[pl.BlockSpec((tm, k), lambda i, j: (i, 0)),
                  pl.BlockSpec((PREV_ROWS, k), lambda i, j: (jnp.maximum(i * (tm // PREV_ROWS) - 1, 0), 0)),
                  pl.BlockSpec((k, tn), lambda i, j: (0, j)),
                  pl.BlockSpec((k, tn), lambda i, j: (0, nw + j)),
                  pl.BlockSpec((k, tn), lambda i, j: (0, 2 * nw + j)),
                  pl.BlockSpec((None, conv_k - 1, tn), lambda i, j: (seq_of(i), 0, j)),
                  pl.BlockSpec((conv_k, tn), lambda i, j: (0, j))],
        out_specs=[pl.BlockSpec((tm, tn), lambda i, j: (i, j)),
                   pl.BlockSpec((None, SUBLANES, tn), lambda i, j: (seq_of(i), 0, j))],
        scratch_shapes=[pltpu.VMEM((SUBLANES + tm, tn), F32)],
        compiler_params=_params(("arbitrary", "arbitrary"), est),
        name="in1_conv",
    )(xn, xn, w2d, w2d, w2d, hist, taps)


def _ssd_body(z_ref, xs_ref, bm_ref, cm_ref, dtc_ref, dtp_ref, h0_ref, hx_ref, hb_ref, hc_ref,
              wx_ref, wb_ref, wc_ref, bx_ref, bb_ref, bc_ref, alc_ref, alp_ref, dbc_ref, dbp_ref, drep_ref, ng_ref,
              ya_ref, hfin_ref, state_sc, xp_sc, *, groups, heads, head_dim, n_state, chunk, conv_k):
    c = pl.program_id(2)
    w, n = heads * head_dim, n_state
    t0 = SUBLANES - (conv_k - 1)

    @pl.when(c == 0)
    def _load_carried():
        state_sc[...] = h0_ref[...]
        for gi in range(groups):
            xp_sc[gi, t0:SUBLANES, 0:w] = hx_ref[:, gi * w:(gi + 1) * w]
            xp_sc[gi, t0:SUBLANES, w:w + n] = hb_ref[:, gi * n:(gi + 1) * n]
            xp_sc[gi, t0:SUBLANES, w + n:w + 2 * n] = hc_ref[:, gi * n:(gi + 1) * n]

    for gi in range(groups):
        xcols, ncols = slice(gi * w, (gi + 1) * w), slice(gi * n, (gi + 1) * n)
        _ssd_group_step(z_ref.at[:, xcols], xs_ref.at[:, xcols], bm_ref.at[:, ncols], cm_ref.at[:, ncols],
                        dtc_ref.at[gi], dtp_ref.at[gi], wx_ref.at[:, xcols], wb_ref.at[:, ncols], wc_ref.at[:, ncols],
                        bx_ref.at[:, xcols], bb_ref.at[:, ncols], bc_ref.at[:, ncols], alc_ref.at[gi], alp_ref.at[gi],
                        dbc_ref.at[gi], dbp_ref.at[gi], drep_ref.at[gi], ng_ref.at[gi], ya_ref.at[:, xcols],
                        state_sc.at[gi], xp_sc.at[gi], heads=heads, head_dim=head_dim, n_state=n_state, chunk=chunk,
                        conv_k=conv_k)

    @pl.when(c == pl.num_programs(2) - 1)
    def _store_state():
        hfin_ref[...] = state_sc[...]


def _ssd_group_step(z_ref, xs_ref, bm_ref, cm_ref, dtc_ref, dtp_ref, wx_ref, wb_ref, wc_ref, bx_ref, bb_ref, bc_ref,
                    alc_ref, alp_ref, dbc_ref, dbp_ref, drep_ref, ng_ref, ya_ref, state_sc, xp_sc, *,
                    heads, head_dim, n_state, chunk, conv_k):
    q, p, n = chunk, head_dim, n_state
    w = heads * p
    t0 = SUBLANES - (conv_k - 1)

    xp_sc[SUBLANES:SUBLANES + q, 0:w] = xs_ref[...]
    xp_sc[SUBLANES:SUBLANES + q, w:w + n] = bm_ref[...]
    xp_sc[SUBLANES:SUBLANES + q, w + n:w + 2 * n] = cm_ref[...]

    def conv_silu(lo, hi, w_ref, b_ref):
        acc = xp_sc[t0:t0 + q, lo:hi] * w_ref[0:1, :]
        for j in range(1, conv_k):
            acc = acc + xp_sc[t0 + j:t0 + j + q, lo:hi] * w_ref[j:j + 1, :]
        return _silu(acc + b_ref[...])

    xs = conv_silu(0, w, wx_ref, bx_ref)
    bm = conv_silu(w, w + n, wb_ref, bb_ref)
    cm = conv_silu(w + n, w + 2 * n, wc_ref, bc_ref)
    xp_sc[t0:SUBLANES, :] = xp_sc[SUBLANES + q - (conv_k - 1):SUBLANES + q, :]

    dt_c = jax.nn.softplus(dtc_ref[...] + dbc_ref[...])
    dt_p = jax.nn.softplus(dtp_ref[...] + dbp_ref[...])
    ii = lax.broadcasted_iota(jnp.int32, (q, q), 0)
    jj = lax.broadcasted_iota(jnp.int32, (q, q), 1)
    acum_c = jnp.dot((ii >= jj).astype(F32), dt_c * -jnp.exp(alc_ref[...]), precision=HIGHEST,
                     preferred_element_type=F32)
    i2 = lax.broadcasted_iota(jnp.int32, (2 * q, 2 * q), 0)
    j2 = lax.broadcasted_iota(jnp.int32, (2 * q, 2 * q), 1)
    same_head = (i2 >= q) == (j2 >= q)
    acum_p = jnp.dot(dt_p * -jnp.exp(alp_ref[...]), ((i2 <= j2) & same_head).astype(F32), precision=HIGHEST,
                     preferred_element_type=F32)

    lane = lax.broadcasted_iota(jnp.int32, (q, LANES), 1)
    first_head = lane < p

    def per_head_lanes(v):
        return jnp.concatenate([jnp.where(first_head, v[:, 2 * k:2 * k + 1], v[:, 2 * k + 1:2 * k + 2])
                                for k in range(heads // 2)], axis=1)

    acum_w = per_head_lanes(acum_c)
    grow_w = jnp.exp(acum_w)
    wend_w = jnp.exp(acum_w[q - 1:q, :] - acum_w) * per_head_lanes(dt_c)

    cm_b = cm.astype(BF16)
    bm_b = bm.astype(BF16)
    scores2 = lax.dot_general(cm_b, jnp.concatenate([bm_b, bm_b], axis=0), NT_DIMS,
                              preferred_element_type=F32)
    row = lax.broadcasted_iota(jnp.int32, (q, 2 * q), 0)
    col2 = lax.broadcasted_iota(jnp.int32, (q, 2 * q), 1)
    causal2 = row >= jnp.where(col2 >= q, col2 - q, col2)
    y_pairs = []
    for k in range(heads // 2):
        seg = acum_w[:, k * LANES:(k + 1) * LANES] - acum_p[k:k + 1, :]
        decay = jnp.exp(jnp.where(causal2, seg, -jnp.inf))
        m2 = (scores2 * decay * dt_p[k:k + 1, :]).astype(BF16)
        x_pair = xs[:, k * LANES:(k + 1) * LANES]
        x2 = jnp.concatenate([jnp.where(first_head, x_pair, 0.0), jnp.where(first_head, 0.0, x_pair)], axis=0)
        y_pairs.append(jnp.dot(m2, x2.astype(BF16), preferred_element_type=F32))
    st = state_sc[...]
    y_off = jnp.dot(cm_b, st.astype(BF16), preferred_element_type=F32)
    xw = (xs * wend_w).astype(BF16)
    state_sc[...] = st * grow_w[q - 1:q, :] + lax.dot_general(bm_b, xw, TN_DIMS, preferred_element_type=F32)

    y = jnp.concatenate(y_pairs, axis=1) + y_off * grow_w + drep_ref[...] * xs
    y = y * _silu(z_ref[...])
    inv = lax.rsqrt(jnp.mean(y * y, axis=-1, keepdims=True) + EPS)
    ya_ref[...] = (y * inv * ng_ref[...]).astype(ya_ref.dtype)


def _ssd_mixer(zx, dt_raw, conv_hist, h0, conv_w, conv_b, a_log, dt_bias, d_skip, norm_g):
    b, l, _ = zx.shape
    _, heads, p, n = h0.shape
    inner = heads * p
    conv_k, conv_dim = conv_w.shape
    g = (conv_dim - inner) // (2 * n)
    r = heads // g
    w = r * p
    q = SSD_CHUNK if l % SSD_CHUNK == 0 else l
    nc = l // q
    assert 2 * p == LANES and r % 2 == 0 and 2 * q == LANES, "two heads / two chunk copies per 128-lane tile"
    dt5 = dt_raw.reshape(b, nc, q, g, r)
    dtc = dt5.transpose(0, 3, 1, 2, 4)
    dtp = dt5.transpose(0, 3, 1, 4, 2).reshape(b, g, nc, r // 2, 2 * q)
    pair_lanes = lambda v: jnp.repeat(v.reshape(g, r // 2, 2), q, axis=-1)
    state0 = h0.reshape(b, g, r, p, n).transpose(0, 1, 4, 2, 3).reshape(b, g, n, w)
    conv_b2 = conv_b.reshape(1, conv_dim)
    gs = SSD_GROUPS_PER_STEP if g % SSD_GROUPS_PER_STEP == 0 else 1
    xw, nw = gs * w, gs * n

    def col(block_w, first_col):
        return pl.BlockSpec((None, q, block_w), lambda bi, gi, ci: (bi, ci, first_col // block_w + gi))

    def hist(block_w, first_col):
        return pl.BlockSpec((None, conv_k - 1, block_w), lambda bi, gi, ci: (bi, 0, first_col // block_w + gi))

    def wrow(rows, block_w, first_col):
        return pl.BlockSpec((rows, block_w), lambda bi, gi, ci: (0, first_col // block_w + gi))

    def per_group(shape):
        return pl.BlockSpec((gs,) + shape, lambda bi, gi, ci: (gi, 0, 0))

    state_spec = pl.BlockSpec((None, gs, n, w), lambda bi, gi, ci: (bi, gi, 0, 0))
    in_specs = [
        col(xw, 0), col(xw, inner), col(nw, 2 * inner), col(nw, 2 * inner + g * n),
        pl.BlockSpec((None, gs, None, q, r), lambda bi, gi, ci: (bi, gi, ci, 0, 0)),
        pl.BlockSpec((None, gs, None, r // 2, 2 * q), lambda bi, gi, ci: (bi, gi, ci, 0, 0)),
        state_spec,
        hist(xw, 0), hist(nw, inner), hist(nw, inner + g * n),
        wrow(conv_k, xw, 0), wrow(conv_k, nw, inner), wrow(conv_k, nw, inner + g * n),
        wrow(1, xw, 0), wrow(1, nw, inner), wrow(1, nw, inner + g * n),
        per_group((1, r)), per_group((r // 2, 2 * q)), per_group((1, r)), per_group((r // 2, 2 * q)),
        per_group((1, w)), per_group((1, w)),
    ]
    est = 4 * gs * (2 * q * (2 * w + 2 * n) + 6 * n * w + (SUBLANES + q) * (w + 2 * n) + 12 * q * w)
    body = functools.partial(_ssd_body, groups=gs, heads=r, head_dim=p, n_state=n, chunk=q, conv_k=conv_k)
    ya, state = pl.pallas_call(
        body,
        out_shape=[jax.ShapeDtypeStruct((b, l, inner), BF16), jax.ShapeDtypeStruct((b, g, n, w), F32)],
        grid=(b, g // gs, nc),
        in_specs=in_specs,
        out_specs=[pl.BlockSpec((None, q, xw), lambda bi, gi, ci: (bi, ci, gi)), state_spec],
        scratch_shapes=[pltpu.VMEM((gs, n, w), F32), pltpu.VMEM((gs, SUBLANES + q, w + 2 * n), F32)],
        compiler_params=_params(("parallel", "parallel", "arbitrary"), est),
        name="ssd_mixer",
    )(zx, zx, zx, zx, dtc, dtp, state0, conv_hist, conv_hist, conv_hist,
      conv_w, conv_w, conv_w, conv_b2, conv_b2, conv_b2,
      a_log.reshape(g, 1, r), pair_lanes(a_log), dt_bias.reshape(g, 1, r), pair_lanes(dt_bias),
      jnp.repeat(d_skip, p).reshape(g, 1, w), norm_g.reshape(g, 1, w))
    h_fin = state.reshape(b, g, n, r, p).transpose(0, 1, 3, 4, 2).reshape(b, heads, p, n)
    return ya, h_fin


def _gelu_tanh(x):
    return 0.5 * x * (1.0 + jnp.tanh(0.7978845608028654 * (x + 0.044715 * (x * x * x))))


def _s5_body(u_ref, h0re_ref, h0im_ref, lam_ref, bblk_ref, cblk_ref, d_ref,
             y_ref, ybf_ref, hre_ref, him_ref, bu_sc, h_sc, *, n_seq, n_pack, t_blk, pitch, n_tiles, unroll):
    j = pl.program_id(1)
    half = n_tiles // 2
    rows = n_seq * t_blk
    cw = u_ref.shape[-1] // n_pack
    sw = half * LANES

    @pl.when(j == 0)
    def _load_state():
        for pk in range(n_pack):
            for k in range(half):
                cols = slice(pk * sw + k * LANES, pk * sw + (k + 1) * LANES)
                h_sc[k, pk * n_seq:(pk + 1) * n_seq, :] = h0re_ref[:, cols]
                h_sc[half + k, pk * n_seq:(pk + 1) * n_seq, :] = h0im_ref[:, cols]

    for pk in range(n_pack):
        u_b = u_ref[:, :, pk * cw:(pk + 1) * cw].reshape(rows, cw).astype(BF16)
        for kk in range(half):
            bu = jnp.dot(u_b, bblk_ref[pk, :, kk * 2 * LANES:(kk + 1) * 2 * LANES], preferred_element_type=F32)
            for s in range(n_seq):
                base = (pk * n_seq + s) * pitch
                bu_sc[2 * kk, base:base + t_blk, :] = bu[s * t_blk:(s + 1) * t_blk, :LANES]
                bu_sc[2 * kk + 1, base:base + t_blk, :] = bu[s * t_blk:(s + 1) * t_blk, LANES:]

    def lam_rows(k):
        return jnp.concatenate([jnp.broadcast_to(lam_ref[pk, :, k * LANES:(k + 1) * LANES], (n_seq, LANES))
                                for pk in range(n_pack)], axis=0)

    lam_re = [lam_rows(k) for k in range(half)]
    lam_im = [lam_rows(half + k) for k in range(half)]

    def step(t, hs):
        idx = pl.ds(t, n_pack * n_seq, stride=pitch)
        new_re, new_im = [], []
        for k in range(half):
            h_re, h_im = hs[k], hs[half + k]
            re_rows, im_rows = bu_sc.at[k], bu_sc.at[half + k]
            n_re = lam_re[k] * h_re - lam_im[k] * h_im + re_rows[idx, :]
            n_im = lam_re[k] * h_im + lam_im[k] * h_re + im_rows[idx, :]
            re_rows[idx, :] = n_re
            im_rows[idx, :] = n_im
            new_re.append(n_re)
            new_im.append(n_im)
        return tuple(new_re + new_im)

    hs = lax.fori_loop(0, t_blk, step, tuple(h_sc[k] for k in range(n_tiles)), unroll=unroll)
    for k in range(n_tiles):
        h_sc[k] = hs[k]

    for pk in range(n_pack):
        h_all = jnp.concatenate(
            [jnp.concatenate([bu_sc[k, (pk * n_seq + s) * pitch:(pk * n_seq + s) * pitch + t_blk, :].astype(BF16)
                              for s in range(n_seq)], axis=0) for k in range(n_tiles)], axis=1)
        u = u_ref[:, :, pk * cw:(pk + 1) * cw].reshape(rows, cw)
        y = jnp.dot(h_all, cblk_ref[pk], preferred_element_type=F32) + d_ref[:, pk * cw:(pk + 1) * cw] * u
        y = _gelu_tanh(y)
        y_ref[:, :, pk * cw:(pk + 1) * cw] = y.reshape(n_seq, t_blk, cw)
        ybf_ref[:, :, pk * cw:(pk + 1) * cw] = y.astype(BF16).reshape(n_seq, t_blk, cw)

    @pl.when(j == pl.num_programs(1) - 1)
    def _store_state():
        for pk in range(n_pack):
            for k in range(half):
                cols = slice(pk * sw + k * LANES, pk * sw + (k + 1) * LANES)
                hre_ref[:, cols] = hs[k][pk * n_seq:(pk + 1) * n_seq, :]
                him_ref[:, cols] = hs[half + k][pk * n_seq:(pk + 1) * n_seq, :]


def _s5_discretize(lam_re, lam_im, log_step, b_re, b_im, c_re, c_im):
    g, p, k = b_re.shape
    gb = S5_GROUP_BLOCK
    nb = g // gb
    step = jnp.exp(log_step)[:, None]
    mag = jnp.exp(lam_re * step)
    lb_re, lb_im = mag * jnp.cos(lam_im * step), mag * jnp.sin(lam_im * step)
    den = lam_re * lam_re + lam_im * lam_im
    f_re = ((lb_re - 1.0) * lam_re + lb_im * lam_im) / den
    f_im = (lb_im * lam_re - (lb_re - 1.0) * lam_im) / den
    bb_re = f_re[..., None] * b_re - f_im[..., None] * b_im
    bb_im = f_re[..., None] * b_im + f_im[..., None] * b_re
    eye = jnp.eye(gb, dtype=F32)
    lam = jnp.stack([lb_re.reshape(nb, gb * p), lb_im.reshape(nb, gb * p)], axis=1).reshape(nb, 1, 2 * gb * p)
    bb = jnp.stack([bb_re, bb_im]).reshape(2, nb, gb, p, k).transpose(1, 2, 4, 0, 3)
    bblk = (bb[:, :, :, :, None, :] * eye[None, :, None, None, :, None]).reshape(nb, gb * k, 2 * gb * p)
    cc = jnp.stack([c_re, -c_im]).reshape(2, nb, gb, k, p).transpose(1, 0, 2, 4, 3)
    cblk = (cc[:, :, :, :, None, :] * eye[None, None, :, None, :, None]).reshape(nb, 2 * gb * p, gb * k)
    return lam, bblk.astype(BF16), cblk.astype(BF16)


def _s5_mixer(u, h0_re, h0_im, lam, bblk, cblk, d_skip):
    b, l, width = u.shape
    _, g, p = h0_re.shape
    nb, cw, sw2 = bblk.shape
    n_tiles = sw2 // LANES
    n_pack = max(1, SUBLANES // b)
    if nb % n_pack:
        n_pack = 1
    t_blk = _largest_divisor(l, (256, 128, 64, 32, 16, 8))
    rows = n_pack * b * t_blk
    pitch = SUBLANES * (t_blk // SUBLANES + (1 - (t_blk // SUBLANES) % 2))
    est = 6 * rows * sw2 + 2 * rows * cw * 10 + 8 * n_pack * cw * sw2 + 6 * rows * cw * 4
    body = functools.partial(_s5_body, n_seq=b, n_pack=n_pack, t_blk=t_blk, pitch=pitch, n_tiles=n_tiles,
                             unroll=S5_SCAN_UNROLL)
    seq_spec = pl.BlockSpec((b, t_blk, n_pack * cw), lambda gi, ti: (0, ti, gi))
    state_spec = pl.BlockSpec((b, n_pack * sw2 // 2), lambda gi, ti: (0, gi))
    y, ybf, h_re, h_im = pl.pallas_call(
        body,
        out_shape=[jax.ShapeDtypeStruct((b, l, width), F32), jax.ShapeDtypeStruct((b, l, width), BF16),
                   jax.ShapeDtypeStruct((b, g * p), F32), jax.ShapeDtypeStruct((b, g * p), F32)],
        grid=(nb // n_pack, l // t_blk),
        in_specs=[seq_spec, state_spec, state_spec,
                  pl.BlockSpec((n_pack, 1, sw2), lambda gi, ti: (gi, 0, 0)),
                  pl.BlockSpec((n_pack, cw, sw2), lambda gi, ti: (gi, 0, 0)),
                  pl.BlockSpec((n_pack, sw2, cw), lambda gi, ti: (gi, 0, 0)),
                  pl.BlockSpec((1, n_pack * cw), lambda gi, ti: (0, gi))],
        out_specs=[seq_spec, seq_spec, state_spec, state_spec],
        scratch_shapes=[pltpu.VMEM((n_tiles, n_pack * b * pitch, LANES), F32),
                        pltpu.VMEM((n_tiles, n_pack * b, LANES), F32)],
        compiler_params=_params(("parallel", "arbitrary"), est),
        name="s5_mixer",
    )(u, h0_re.reshape(b, g * p), h0_im.reshape(b, g * p), lam, bblk, cblk, d_skip.reshape(1, width))
    return y, ybf, h_re.reshape(b, g, p), h_im.reshape(b, g, p)


def _short_conv_body(b_ref, c_ref, v_ref, hist_ref, w_ref, o_ref, hnew_ref, cv_sc, *, t_blk, conv_k):
    t = pl.program_id(2)
    t0 = SUBLANES - (conv_k - 1)

    @pl.when(t == 0)
    def _load_hist():
        cv_sc[t0:SUBLANES, :] = hist_ref[...]

    cv_sc[SUBLANES:SUBLANES + t_blk, :] = c_ref[...] * v_ref[...]
    acc = cv_sc[t0:t0 + t_blk, :] * w_ref[0:1, :]
    for j in range(1, conv_k):
        acc = acc + cv_sc[t0 + j:t0 + j + t_blk, :] * w_ref[j:j + 1, :]
    o_ref[...] = (b_ref[...] * acc).astype(o_ref.dtype)
    cv_sc[t0:SUBLANES, :] = cv_sc[SUBLANES + t_blk - (conv_k - 1):SUBLANES + t_blk, :]

    @pl.when(t == pl.num_programs(2) - 1)
    def _store_hist():
        hnew_ref[...] = cv_sc[t0:SUBLANES, :]


def _short_conv(bcv, hist, conv_w):
    bsz, l, width3 = bcv.shape
    width = width3 // 3
    conv_k = conv_w.shape[0]
    t_blk = _largest_divisor(l, (256, 128, 64, 32, 16, 8))
    tw = _largest_divisor(width, (1024, 512, 256, 128))
    nw = width // tw

    def col(first):
        return pl.BlockSpec((None, t_blk, tw), lambda bi, wi, ti: (bi, ti, first + wi))

    body = functools.partial(_short_conv_body, t_blk=t_blk, conv_k=conv_k)
    return pl.pallas_call(
        body,
        out_shape=[jax.ShapeDtypeStruct((bsz, l, width), BF16), jax.ShapeDtypeStruct(hist.shape, F32)],
        grid=(bsz, nw, l // t_blk),
        in_specs=[col(0), col(nw), col(2 * nw),
                  pl.BlockSpec((None, conv_k - 1, tw), lambda bi, wi, ti: (bi, 0, wi)),
                  pl.BlockSpec((conv_k, tw), lambda bi, wi, ti: (0, wi))],
        out_specs=[pl.BlockSpec((None, t_blk, tw), lambda bi, wi, ti: (bi, ti, wi)),
                   pl.BlockSpec((None, conv_k - 1, tw), lambda bi, wi, ti: (bi, 0, wi))],
        scratch_shapes=[pltpu.VMEM((SUBLANES + t_blk, tw), F32)],
        compiler_params=_params(("parallel", "parallel", "arbitrary"), 4 * t_blk * tw * 12),
        name="short_conv",
    )(bcv, bcv, bcv, hist, conv_w)


def _prepare_weights(w):
    heads, p, _ = w["ssd_shape"]
    inner = heads * p
    conv_dim = w["ssd_conv_w"].shape[-1]
    out = {}
    assert w["w_out0"].shape[1] == 2 * inner
    out["w_u"] = w["w_in0"][:, :, inner + conv_dim + heads:].astype(BF16)
    out["s5"] = [_s5_discretize(w["s5_lam_re"][j], w["s5_lam_im"][j], w["s5_log_step"][j], w["s5_b_re"][j],
                                w["s5_b_im"][j], w["s5_c_re"][j], w["s5_c_im"][j])
                 for j in range(w["s5_lam_re"].shape[0])]
    return out


def _run_trunk(x3, p4, conv_ssd, h_ssd, s5_re, s5_im, conv_sc, w, wb, emitted):
    bsz, l, d = x3.shape
    m = bsz * l
    depth = w["g_ffn1"].shape[0]
    heads = h_ssd.shape[2]
    x = x3.reshape(m, d)
    inner = heads * h_ssd.shape[3]
    zx_cols = inner + w["ssd_conv_w"].shape[-1]
    conv_in_projection = emitted is None and l % CONV_MM_ROWS == 0

    def f32_window(name, i):
        if name == "w_zx":
            return _LayerWeight(w["w_in0"], i, n=zx_cols)
        if name == "w_dt":
            return _LayerWeight(w["w_in0"], i, col_start=zx_cols, n=_round_up(heads, LANES))
        if name in ("w_out0_a", "w_out0_b"):
            return _LayerWeight(w["w_out0"], i, k=inner, row_block=0 if name == "w_out0_a" else 1)
        return _LayerWeight(w[name], i)

    def mm(a_list, names, i, dots, extras, epilogue, out_dtype, tag):
        if names[0] in wb:
            ws = [wb[nm][i] if isinstance(wb[nm], dict) else _LayerWeight(wb[nm], i) for nm in names]
            return _fused_matmul(a_list, ws, dots, extras, epilogue, out_dtype, tag)
        out, copies = _fused_matmul(a_list, [f32_window(nm, i) for nm in names], dots, extras, epilogue,
                                    out_dtype, tag, emit_weights=True)
        for nm, copy in zip(names, copies):
            emitted.setdefault(nm, {})[i] = copy
        return out

    def ffn_half_step(x, g, i, tag):
        xn = _rmsnorm(x, g, BF16)
        h = mm([xn], [f"w_{tag}_gate", f"w_{tag}_up"], i, [(0, 0), (0, 1)], [], _ep_swiglu, BF16, f"{tag}_gate_up")
        return mm([h], [f"w_{tag}_down"], i, [(0, 0)], [x], _ep_half_residual, F32, f"{tag}_down")

    out_conv_ssd, out_h, out_re, out_im, out_sc = [], [], [], [], []
    for i in range(depth):
        x = ffn_half_step(x, w["g_ffn1"][i], i, "ffn1")
        xn = _rmsnorm(x, w["g_mix"][i], BF16)
        j = i // 2
        if i % 2 == 0:
            conv_k, conv_dim = w["ssd_conv_w"][j].shape
            zx = mm([xn], ["w_zx"], j, [(0, 0)], [], _ep_plain, F32, "in0_zxbc")
            dt = mm([xn], ["w_dt"], j, [(0, 0)], [], _ep_plain, F32, "in0_dt")[:, :heads]
            u = mm([xn], ["w_u"], j, [(0, 0)], [], _ep_plain, F32, "in0_u")
            zx3 = zx.reshape(bsz, l, zx.shape[-1])
            ya, h_new = _ssd_mixer(zx3, dt.reshape(bsz, l, heads), conv_ssd[j], h_ssd[j], w["ssd_conv_w"][j],
                                   w["ssd_conv_b"][j], w["ssd_a_log"][j], w["ssd_dt_bias"][j], w["ssd_d"][j],
                                   w["ssd_norm_g"][j])
            xbc_tail = zx3[:, l - min(l, conv_k - 1):, inner:inner + conv_dim]
            c_new = jnp.concatenate([conv_ssd[j], xbc_tail], axis=1)[:, -(conv_k - 1):]
            lam, bblk, cblk = wb["s5"][j]
            yb, yb_bf, re_new, im_new = _s5_mixer(u.reshape(bsz, l, u.shape[-1]), s5_re[j], s5_im[j], lam, bblk, cblk,
                                                  w["s5_d"][j])
            width = yb.shape[-1]
            yb_glu = mm([yb_bf.reshape(m, width)], ["s5_w_glu"], j, [(0, 0)],
                        [yb.reshape(m, width), w["s5_b_glu"][j].reshape(1, width)], _ep_glu, BF16, "s5_glu")
            x = mm([ya.reshape(m, inner), yb_glu], ["w_out0_a", "w_out0_b"], j, [(0, 0), (1, 1)], [x],
                   _ep_residual2, F32, "out0")
            out_conv_ssd.append(c_new)
            out_h.append(h_new)
            out_re.append(re_new)
            out_im.append(im_new)
        else:
            if conv_in_projection:
                taps = w["sc_conv_w"][j]
                yc, tail = _short_conv_projection(xn, wb["w_in1"][j], conv_sc[j], taps, l)
                sc_new = tail[:, SUBLANES - (taps.shape[0] - 1):, :]
            else:
                bcv = mm([xn], ["w_in1"], j, [(0, 0)], [], _ep_plain, F32, "in1")
                yc, sc_new = _short_conv(bcv.reshape(bsz, l, bcv.shape[-1]), conv_sc[j], w["sc_conv_w"][j])
            x = mm([yc.reshape(m, yc.shape[-1])], ["w_out1"], j, [(0, 0)], [x], _ep_residual, F32, "out1")
            out_sc.append(sc_new)
        x = ffn_half_step(x, w["g_ffn2"][i], i, "ffn2")
        xn = _rmsnorm(x, w["g_pe"][i], BF16)
        x = mm([xn, p4[i].reshape(m, p4.shape[-1])], ["w_pe_gate", "w_pe_proj"], i, [(0, 0), (1, 1)], [x],
               _ep_gated_embed, F32, "pe_embed")
    y = _rmsnorm(x, w["g_final"], F32).reshape(bsz, l, d)
    return (y, jnp.stack(out_conv_ssd), jnp.stack(out_h), jnp.stack(out_re), jnp.stack(out_im), jnp.stack(out_sc))


def kernel(x_prompt, x_sample, p_prompt, p_sample, cache_conv_ssd, state_ssd, state_s5_re, state_s5_im, cache_conv_short, g_ffn1, w_ffn1_gate, w_ffn1_up, w_ffn1_down, g_mix, g_ffn2, w_ffn2_gate, w_ffn2_up, w_ffn2_down, g_pe, w_pe_gate, w_pe_proj, w_in0, ssd_conv_w, ssd_conv_b, ssd_a_log, ssd_dt_bias, ssd_d, ssd_norm_g, s5_lam_re, s5_lam_im, s5_log_step, s5_b_re, s5_b_im, s5_c_re, s5_c_im, s5_d, s5_w_glu, s5_b_glu, w_out0, w_in1, sc_conv_w, w_out1, g_final):
    w = dict(g_ffn1=g_ffn1, w_ffn1_gate=w_ffn1_gate, w_ffn1_up=w_ffn1_up, w_ffn1_down=w_ffn1_down,
             g_mix=g_mix, g_ffn2=g_ffn2, w_ffn2_gate=w_ffn2_gate, w_ffn2_up=w_ffn2_up, w_ffn2_down=w_ffn2_down,
             g_pe=g_pe, w_pe_gate=w_pe_gate, w_pe_proj=w_pe_proj, w_in0=w_in0, ssd_conv_w=ssd_conv_w,
             ssd_conv_b=ssd_conv_b, ssd_a_log=ssd_a_log, ssd_dt_bias=ssd_dt_bias, ssd_d=ssd_d,
             ssd_norm_g=ssd_norm_g, s5_lam_re=s5_lam_re, s5_lam_im=s5_lam_im, s5_log_step=s5_log_step,
             s5_b_re=s5_b_re, s5_b_im=s5_b_im, s5_c_re=s5_c_re, s5_c_im=s5_c_im, s5_d=s5_d,
             s5_w_glu=s5_w_glu, s5_b_glu=s5_b_glu, w_out0=w_out0, w_in1=w_in1, sc_conv_w=sc_conv_w,
             w_out1=w_out1, g_final=g_final, ssd_shape=state_ssd.shape[2:])
    wb = _prepare_weights(w)
    emitted = {}
    sample = _run_trunk(x_sample, p_sample, cache_conv_ssd, state_ssd, state_s5_re, state_s5_im, cache_conv_short, w,
                        wb, emitted)
    wb.update(emitted)
    b = x_prompt.shape[0]
    zeros = lambda ref: jnp.zeros((ref.shape[0], b) + ref.shape[2:], F32)
    prompt = _run_trunk(x_prompt, p_prompt, zeros(cache_conv_ssd), zeros(state_ssd), zeros(state_s5_re),
                        zeros(state_s5_im), zeros(cache_conv_short), w, wb, None)
    return (prompt[0], sample[0]) + prompt[1:] + sample[1:]
```

```python
import functools

import jax
import jax.numpy as jnp
from jax import lax
from jax.experimental import pallas as pl
from jax.experimental.pallas import tpu as pltpu

F32 = jnp.float32
BF16 = jnp.bfloat16
EPS = 1e-6
SSD_CHUNK = 64
SSD_GROUPS_PER_STEP = 4
S5_GROUP_BLOCK = 16
S5_SCAN_UNROLL = 8
CONV_MM_ROWS = 1024
PREV_ROWS = 16
LANES = 128
SUBLANES = 8
V7X_VMEM_BYTES = 64 * 1024 * 1024
MM_VMEM_BUDGET = 48 * 1024 * 1024
MXU_WIDTH = 256
MXU_FLOPS = 0.95e15
HBM_BYTES_PER_S = 3.0e12
GRID_STEP_S = 0.35e-6
HIGHEST = lax.Precision.HIGHEST
NT_DIMS = (((1,), (1,)), ((), ()))
TN_DIMS = (((0,), (0,)), ((), ()))


def _largest_divisor(n, candidates):
    for c in candidates:
        if n % c == 0:
            return c
    raise ValueError(f"no tile in {candidates} divides {n}")


def _round_up(n, m):
    return (n + m - 1) // m * m


def _params(semantics, vmem_estimate):
    limit = min(V7X_VMEM_BYTES - 8 * 1024 * 1024, int(vmem_estimate * 1.25) + 8 * 1024 * 1024)
    return pltpu.CompilerParams(dimension_semantics=semantics, vmem_limit_bytes=limit)


def _silu(x):
    return x * jax.nn.sigmoid(x)


def _rmsnorm_body(x_ref, g_ref, o_ref):
    x = x_ref[...]
    inv = lax.rsqrt(jnp.mean(x * x, axis=-1, keepdims=True) + EPS)
    o_ref[...] = (x * inv * g_ref[...]).astype(o_ref.dtype)


def _rmsnorm(x, g, out_dtype):
    m, d = x.shape
    tm = _largest_divisor(m, (256, 128, 64, 32, 16, 8))
    return pl.pallas_call(
        _rmsnorm_body,
        out_shape=jax.ShapeDtypeStruct((m, d), out_dtype),
        grid=(m // tm,),
        in_specs=[pl.BlockSpec((tm, d), lambda i: (i, 0)), pl.BlockSpec((1, d), lambda i: (0, 0))],
        out_specs=pl.BlockSpec((tm, d), lambda i: (i, 0)),
        compiler_params=_params(("parallel",), 2 * tm * d * (4 + jnp.dtype(out_dtype).itemsize) + 3 * tm * d * 4),
        name="rmsnorm",
    )(x, g.reshape(1, d))


def _mm_body(*refs, n_a, n_w, n_e, dots, epilogue):
    a_refs, w_refs, e_refs = refs[:n_a], refs[n_a:n_a + n_w], refs[n_a + n_w:n_a + n_w + n_e]
    o_ref, w_out_refs = refs[n_a + n_w + n_e], refs[n_a + n_w + n_e + 1:]
    w_vals = [w[...] if w.dtype == BF16 else w[...].astype(BF16) for w in w_refs]
    a_vals = {}
    accs = []
    for ai, wi in dots:
        if ai not in a_vals:
            a_vals[ai] = a_refs[ai][...].astype(BF16)
        accs.append(jnp.dot(a_vals[ai], w_vals[wi], preferred_element_type=F32))
    o_ref[...] = epilogue(accs, [e[...] for e in e_refs]).astype(o_ref.dtype)
    for w_out, w_val in zip(w_out_refs, w_vals):
        w_out[...] = w_val


class _LayerWeight:
    def __init__(self, stack, layer, k=None, row_block=0, col_start=0, n=None):
        self.stack, self.layer, self.row_block, self.col_start = stack, layer, row_block, col_start
        self.shape = (k or stack.shape[1], n or stack.shape[2])
        self.dtype = stack.dtype
        self.size = self.shape[0] * self.shape[1]

    def allows(self, tn):
        return self.col_start % tn == 0

    def block_spec(self, tn):
        first = self.col_start // tn
        return pl.BlockSpec((None, self.shape[0], tn), lambda i, j: (self.layer, self.row_block, first + j))


def _mm_vmem(tm, tn, a_list, w_list, n_dots, tiles, out_dtype, emit_weights):
    a_bytes = sum(tm * a.shape[1] * a.dtype.itemsize for a in a_list)
    w_bytes = sum(w.shape[0] * tn * (w.dtype.itemsize + (2 if emit_weights else 0)) for w in w_list)
    io_bytes = tm * tn * (jnp.dtype(out_dtype).itemsize + 4 * tiles)
    return 2 * (a_bytes + w_bytes + io_bytes) + (n_dots + 1) * tm * tn * 4


def _fused_matmul(a_list, w_list, dots, extras, epilogue, out_dtype, name, emit_weights=False):
    m, n = a_list[0].shape[0], w_list[0].shape[1]
    n_tiles = sum(1 for e in extras if e.shape[0] != 1)
    flops = 2.0 * m * n * sum(w_list[wi].shape[0] for _, wi in dots)
    a_bytes = sum(a.size * a.dtype.itemsize for a in a_list)
    w_bytes = sum(w.size * (w.dtype.itemsize + (2 if emit_weights else 0)) for w in w_list)
    io_bytes = m * n * (jnp.dtype(out_dtype).itemsize + 4 * n_tiles)
    choice = None
    for tm in (1024, 512, 256, 128, 64, 32, 16, 8):
        for tn in (1024, 512, 256, 128):
            if m % tm or tn > n or (emit_weights and (tm != m or n % tn)):
                continue
            if not all(w.allows(tn) for w in w_list if isinstance(w, _LayerWeight)):
                continue
            est = _mm_vmem(tm, tn, a_list, w_list, len(dots), n_tiles, out_dtype, emit_weights)
            if est > MM_VMEM_BUDGET:
                continue
            col_blocks = pl.cdiv(n, tn)
            t_mxu = (flops * col_blocks * tn / n) / MXU_FLOPS * (1.0 if tn % MXU_WIDTH == 0 and tm % MXU_WIDTH == 0 else 2.0)
            t_hbm = (a_bytes + (m // tm) * w_bytes + io_bytes) / HBM_BYTES_PER_S
            cost = max(t_mxu, t_hbm) + (m // tm) * col_blocks * GRID_STEP_S
            if choice is None or cost < choice[0]:
                choice = (cost, tm, tn, est)
    if choice is None:
        raise ValueError(f"{name}: no matmul tiling fits VMEM")
    _, tm, tn, est = choice
    in_specs = [pl.BlockSpec((tm, a.shape[1]), lambda i, j: (i, 0)) for a in a_list]
    in_specs += [w.block_spec(tn) if isinstance(w, _LayerWeight) else pl.BlockSpec((w.shape[0], tn), lambda i, j: (0, j))
                 for w in w_list]
    for e in extras:
        if e.shape[0] == 1:
            in_specs.append(pl.BlockSpec((1, tn), lambda i, j: (0, j)))
        else:
            in_specs.append(pl.BlockSpec((tm, tn), lambda i, j: (i, j)))
    body = functools.partial(_mm_body, n_a=len(a_list), n_w=len(w_list), n_e=len(extras), dots=tuple(dots),
                             epilogue=epilogue)
    out_shape = [jax.ShapeDtypeStruct((m, n), out_dtype)]
    out_specs = [pl.BlockSpec((tm, tn), lambda i, j: (i, j))]
    if emit_weights:
        assert m == tm
        out_shape += [jax.ShapeDtypeStruct(w.shape, BF16) for w in w_list]
        out_specs += [pl.BlockSpec((w.shape[0], tn), lambda i, j: (0, j)) for w in w_list]
    outs = pl.pallas_call(
        body,
        out_shape=out_shape,
        grid=(m // tm, pl.cdiv(n, tn)),
        in_specs=in_specs,
        out_specs=out_specs,
        compiler_params=_params(("parallel", "parallel"), est),
        name=name,
    )(*a_list, *[w.stack if isinstance(w, _LayerWeight) else w for w in w_list], *extras)
    return (outs[0], outs[1:]) if emit_weights else outs[0]


def _shifted_columns_body(lo_ref, hi_ref, o_ref, *, shift):
    width = o_ref.shape[1]
    keep = width - shift
    lane = lax.broadcasted_iota(jnp.int32, o_ref.shape, 1)
    o_ref[...] = jnp.where(lane < keep, pltpu.roll(lo_ref[...], keep, 1), pltpu.roll(hi_ref[...], keep, 1))


def _shifted_columns(stack, layer, col_start, n):
    k, cols = stack.shape[1:]
    bw = _largest_divisor(n, (512, 256, 128))
    first, shift = divmod(col_start, bw)
    assert (first + n // bw + 1) * bw <= cols
    tk = _largest_divisor(k, (1024, 512, 256, 128, 64, 32, 16))
    item = stack.dtype.itemsize
    return pl.pallas_call(
        functools.partial(_shifted_columns_body, shift=shift),
        out_shape=jax.ShapeDtypeStruct((k, n), stack.dtype),
        grid=(k // tk, n // bw),
        in_specs=[pl.BlockSpec((None, tk, bw), lambda i, j: (layer, i, first + j)),
                  pl.BlockSpec((None, tk, bw), lambda i, j: (layer, i, first + j + 1))],
        out_specs=pl.BlockSpec((tk, bw), lambda i, j: (i, j)),
        compiler_params=_params(("parallel", "parallel"), 9 * tk * bw * item),
        name="shifted_columns",
    )(stack, stack)


def _ep_plain(accs, ex):
    return accs[0]


def _ep_swiglu(accs, ex):
    return _silu(accs[0]) * accs[1]


def _ep_half_residual(accs, ex):
    return ex[0] + 0.5 * accs[0]


def _ep_residual(accs, ex):
    return ex[0] + accs[0]


def _ep_residual2(accs, ex):
    return ex[0] + (accs[0] + accs[1])


def _ep_glu(accs, ex):
    return ex[0] * jax.nn.sigmoid(accs[0] + ex[1])


def _ep_gated_embed(accs, ex):
    return ex[0] + jax.nn.sigmoid(accs[0]) * accs[1]


def _causal_taps(sc, cur, prev, taps_ref, conv_k):
    tm = cur.shape[0]
    t0 = SUBLANES - (conv_k - 1)
    sc[t0:SUBLANES, :] = prev
    sc[SUBLANES:SUBLANES + tm, :] = cur
    out = cur * taps_ref[conv_k - 1:conv_k, :]
    for j in range(conv_k - 1):
        out = out + sc[t0 + j:t0 + j + tm, :] * taps_ref[j:j + 1, :]
    return out


def _in1_conv_body(a_ref, aprev_ref, wb_ref, wc_ref, wv_ref, hist_ref, taps_ref, o_ref, tail_ref, sc, *,
                   conv_k, blocks_per_seq):
    i = pl.program_id(0)
    tm = a_ref.shape[0]
    a, ap = a_ref[...], aprev_ref[...]
    wc, wv = wc_ref[...], wv_ref[...]
    cv = jnp.dot(a, wc, preferred_element_type=F32) * jnp.dot(a, wv, preferred_element_type=F32)
    tail_ref[...] = cv[tm - SUBLANES:, :]
    prev = jnp.dot(ap, wc, preferred_element_type=F32) * jnp.dot(ap, wv, preferred_element_type=F32)
    prev = jnp.where(i % blocks_per_seq == 0, hist_ref[...], prev[PREV_ROWS - (conv_k - 1):, :])
    y = _causal_taps(sc, cv, prev, taps_ref, conv_k)
    o_ref[...] = (jnp.dot(a, wb_ref[...], preferred_element_type=F32) * y).astype(o_ref.dtype)


def _short_conv_projection(xn, w2d, hist, taps, seq_len):
    m, k = xn.shape
    n = w2d.shape[1] // 3
    tm = CONV_MM_ROWS
    tn = _largest_divisor(n, (256, 128))
    nw = n // tn
    blocks_per_seq = seq_len // tm
    conv_k = taps.shape[0]
    seq_of = lambda i: i // blocks_per_seq
    est = 2 * (tm * k * 2 + 3 * k * tn * 2 + tm * tn * 4) + (SUBLANES + tm) * tn * 4 + 4 * tm * tn * 4
    return pl.pallas_call(
        functools.partial(_in1_conv_body, conv_k=conv_k, blocks_per_seq=blocks_per_seq),
        out_shape=[jax.ShapeDtypeStruct((m, n), BF16), jax.ShapeDtypeStruct((m // seq_len, SUBLANES, n), F32)],
        grid=(m // tm, n // tn),
        in_specs=[pl.BlockSpec((tm, k), lambda i, j: (i, 0)),
                  pl.BlockSpec((PREV_ROWS, k), lambda i, j: (jnp.maximum(i * (tm // PREV_ROWS) - 1, 0), 0)),
                  pl.BlockSpec((k, tn), lambda i, j: (0, j)),
                  pl.BlockSpec((k, tn), lambda i, j: (0, nw + j)),
                  pl.BlockSpec((k, tn), lambda i, j: (0, 2 * nw + j)),
                  pl.BlockSpec((None, conv_k - 1, tn), lambda i, j: (seq_of(i), 0, j)),
                  pl.BlockSpec((conv_k, tn), lambda i, j: (0, j))],
        out_specs=[pl.BlockSpec((tm, tn), lambda i, j: (i, j)),
                   pl.BlockSpec((None, SUBLANES, tn), lambda i, j: (seq_of(i), 0, j))],
        scratch_shapes=[pltpu.VMEM((SUBLANES + tm, tn), F32)],
        compiler_params=_params(("arbitrary", "arbitrary"), est),
        name="in1_conv",
    )(xn, xn, w2d, w2d, w2d, hist, taps)


def _ssd_body(z_ref, xs_ref, bm_ref, cm_ref, dtc_ref, dtp_ref, h0_ref, hx_ref, hb_ref, hc_ref,
              wx_ref, wb_ref, wc_ref, bx_ref, bb_ref, bc_ref, alc_ref, alp_ref, dbc_ref, dbp_ref, drep_ref, ng_ref,
              ya_ref, hfin_ref, state_sc, xp_sc, *, groups, heads, head_dim, n_state, chunk, conv_k):
    c = pl.program_id(2)
    w, n = heads * head_dim, n_state
    t0 = SUBLANES - (conv_k - 1)

    @pl.when(c == 0)
    def _load_carried():
        state_sc[...] = h0_ref[...]
        for gi in range(groups):
            xp_sc[gi, t0:SUBLANES, 0:w] = hx_ref[:, gi * w:(gi + 1) * w]
            xp_sc[gi, t0:SUBLANES, w:w + n] = hb_ref[:, gi * n:(gi + 1) * n]
            xp_sc[gi, t0:SUBLANES, w + n:w + 2 * n] = hc_ref[:, gi * n:(gi + 1) * n]

    for gi in range(groups):
        xcols, ncols = slice(gi * w, (gi + 1) * w), slice(gi * n, (gi + 1) * n)
        _ssd_group_step(z_ref.at[:, xcols], xs_ref.at[:, xcols], bm_ref.at[:, ncols], cm_ref.at[:, ncols],
                        dtc_ref.at[gi], dtp_ref.at[gi], wx_ref.at[:, xcols], wb_ref.at[:, ncols], wc_ref.at[:, ncols],
                        bx_ref.at[:, xcols], bb_ref.at[:, ncols], bc_ref.at[:, ncols], alc_ref.at[gi], alp_ref.at[gi],
                        dbc_ref.at[gi], dbp_ref.at[gi], drep_ref.at[gi], ng_ref.at[gi], ya_ref.at[:, xcols],
                        state_sc.at[gi], xp_sc.at[gi], heads=heads, head_dim=head_dim, n_state=n_state, chunk=chunk,
                        conv_k=conv_k)

    @pl.when(c == pl.num_programs(2) - 1)
    def _store_state():
        hfin_ref[...] = state_sc[...]


def _ssd_group_step(z_ref, xs_ref, bm_ref, cm_ref, dtc_ref, dtp_ref, wx_ref, wb_ref, wc_ref, bx_ref, bb_ref, bc_ref,
                    alc_ref, alp_ref, dbc_ref, dbp_ref, drep_ref, ng_ref, ya_ref, state_sc, xp_sc, *,
                    heads, head_dim, n_state, chunk, conv_k):
    q, p, n = chunk, head_dim, n_state
    w = heads * p
    t0 = SUBLANES - (conv_k - 1)

    xp_sc[SUBLANES:SUBLANES + q, 0:w] = xs_ref[...]
    xp_sc[SUBLANES:SUBLANES + q, w:w + n] = bm_ref[...]
    xp_sc[SUBLANES:SUBLANES + q, w + n:w + 2 * n] = cm_ref[...]

    def conv_silu(lo, hi, w_ref, b_ref):
        acc = xp_sc[t0:t0 + q, lo:hi] * w_ref[0:1, :]
        for j in range(1, conv_k):
            acc = acc + xp_sc[t0 + j:t0 + j + q, lo:hi] * w_ref[j:j + 1, :]
        return _silu(acc + b_ref[...])

    xs = conv_silu(0, w, wx_ref, bx_ref)
    bm = conv_silu(w, w + n, wb_ref, bb_ref)
    cm = conv_silu(w + n, w + 2 * n, wc_ref, bc_ref)
    xp_sc[t0:SUBLANES, :] = xp_sc[SUBLANES + q - (conv_k - 1):SUBLANES + q, :]

    dt_c = jax.nn.softplus(dtc_ref[...] + dbc_ref[...])
    dt_p = jax.nn.softplus(dtp_ref[...] + dbp_ref[...])
    ii = lax.broadcasted_iota(jnp.int32, (q, q), 0)
    jj = lax.broadcasted_iota(jnp.int32, (q, q), 1)
    acum_c = jnp.dot((ii >= jj).astype(F32), dt_c * -jnp.exp(alc_ref[...]), precision=HIGHEST,
                     preferred_element_type=F32)
    i2 = lax.broadcasted_iota(jnp.int32, (2 * q, 2 * q), 0)
    j2 = lax.broadcasted_iota(jnp.int32, (2 * q, 2 * q), 1)
    same_head = (i2 >= q) == (j2 >= q)
    acum_p = jnp.dot(dt_p * -jnp.exp(alp_ref[...]), ((i2 <= j2) & same_head).astype(F32), precision=HIGHEST,
                     preferred_element_type=F32)

    lane = lax.broadcasted_iota(jnp.int32, (q, LANES), 1)
    first_head = lane < p

    def per_head_lanes(v):
        return jnp.concatenate([jnp.where(first_head, v[:, 2 * k:2 * k + 1], v[:, 2 * k + 1:2 * k + 2])
                                for k in range(heads // 2)], axis=1)

    acum_w = per_head_lanes(acum_c)
    grow_w = jnp.exp(acum_w)
    wend_w = jnp.exp(acum_w[q - 1:q, :] - acum_w) * per_head_lanes(dt_c)

    cm_b = cm.astype(BF16)
    bm_b = bm.astype(BF16)
    scores2 = lax.dot_general(cm_b, jnp.concatenate([bm_b, bm_b], axis=0), NT_DIMS,
                              preferred_element_type=F32)
    row = lax.broadcasted_iota(jnp.int32, (q, 2 * q), 0)
    col2 = lax.broadcasted_iota(jnp.int32, (q, 2 * q), 1)
    causal2 = row >= jnp.where(col2 >= q, col2 - q, col2)
    y_pairs = []
    for k in range(heads // 2):
        seg = acum_w[:, k * LANES:(k + 1) * LANES] - acum_p[k:k + 1, :]
        decay = jnp.exp(jnp.where(causal2, seg, -jnp.inf))
        m2 = (scores2 * decay * dt_p[k:k + 1, :]).astype(BF16)
        x_pair = xs[:, k * LANES:(k + 1) * LANES]
        x2 = jnp.concatenate([jnp.where(first_head, x_pair, 0.0), jnp.where(first_head, 0.0, x_pair)], axis=0)
        y_pairs.append(jnp.dot(m2, x2.astype(BF16), preferred_element_type=F32))
    st = state_sc[...]
    y_off = jnp.dot(cm_b, st.astype(BF16), preferred_element_type=F32)
    xw = (xs * wend_w).astype(BF16)
    state_sc[...] = st * grow_w[q - 1:q, :] + lax.dot_general(bm_b, xw, TN_DIMS, preferred_element_type=F32)

    y = jnp.concatenate(y_pairs, axis=1) + y_off * grow_w + drep_ref[...] * xs
    y = y * _silu(z_ref[...])
    inv = lax.rsqrt(jnp.mean(y * y, axis=-1, keepdims=True) + EPS)
    ya_ref[...] = (y * inv * ng_ref[...]).astype(ya_ref.dtype)


def _ssd_mixer(zx, dt_raw, conv_hist, h0, conv_w, conv_b, a_log, dt_bias, d_skip, norm_g):
    b, l, _ = zx.shape
    _, heads, p, n = h0.shape
    inner = heads * p
    conv_k, conv_dim = conv_w.shape
    g = (conv_dim - inner) // (2 * n)
    r = heads // g
    w = r * p
    q = SSD_CHUNK if l % SSD_CHUNK == 0 else l
    nc = l // q
    assert 2 * p == LANES and r % 2 == 0 and 2 * q == LANES, "two heads / two chunk copies per 128-lane tile"
    dt5 = dt_raw.reshape(b, nc, q, g, r)
    dtc = dt5.transpose(0, 3, 1, 2, 4)
    dtp = dt5.transpose(0, 3, 1, 4, 2).reshape(b, g, nc, r // 2, 2 * q)
    pair_lanes = lambda v: jnp.repeat(v.reshape(g, r // 2, 2), q, axis=-1)
    state0 = h0.reshape(b, g, r, p, n).transpose(0, 1, 4, 2, 3).reshape(b, g, n, w)
    conv_b2 = conv_b.reshape(1, conv_dim)
    gs = SSD_GROUPS_PER_STEP if g % SSD_GROUPS_PER_STEP == 0 else 1
    xw, nw = gs * w, gs * n

    def col(block_w, first_col):
        return pl.BlockSpec((None, q, block_w), lambda bi, gi, ci: (bi, ci, first_col // block_w + gi))

    def hist(block_w, first_col):
        return pl.BlockSpec((None, conv_k - 1, block_w), lambda bi, gi, ci: (bi, 0, first_col // block_w + gi))

    def wrow(rows, block_w, first_col):
        return pl.BlockSpec((rows, block_w), lambda bi, gi, ci: (0, first_col // block_w + gi))

    def per_group(shape):
        return pl.BlockSpec((gs,) + shape, lambda bi, gi, ci: (gi, 0, 0))

    state_spec = pl.BlockSpec((None, gs, n, w), lambda bi, gi, ci: (bi, gi, 0, 0))
    in_specs = [
        col(xw, 0), col(xw, inner), col(nw, 2 * inner), col(nw, 2 * inner + g * n),
        pl.BlockSpec((None, gs, None, q, r), lambda bi, gi, ci: (bi, gi, ci, 0, 0)),
        pl.BlockSpec((None, gs, None, r // 2, 2 * q), lambda bi, gi, ci: (bi, gi, ci, 0, 0)),
        state_spec,
        hist(xw, 0), hist(nw, inner), hist(nw, inner + g * n),
        wrow(conv_k, xw, 0), wrow(conv_k, nw, inner), wrow(conv_k, nw, inner + g * n),
        wrow(1, xw, 0), wrow(1, nw, inner), wrow(1, nw, inner + g * n),
        per_group((1, r)), per_group((r // 2, 2 * q)), per_group((1, r)), per_group((r // 2, 2 * q)),
        per_group((1, w)), per_group((1, w)),
    ]
    est = 4 * gs * (2 * q * (2 * w + 2 * n) + 6 * n * w + (SUBLANES + q) * (w + 2 * n) + 12 * q * w)
    body = functools.partial(_ssd_body, groups=gs, heads=r, head_dim=p, n_state=n, chunk=q, conv_k=conv_k)
    ya, state = pl.pallas_call(
        body,
        out_shape=[jax.ShapeDtypeStruct((b, l, inner), BF16), jax.ShapeDtypeStruct((b, g, n, w), F32)],
        grid=(b, g // gs, nc),
        in_specs=in_specs,
        out_specs=[pl.BlockSpec((None, q, xw), lambda bi, gi, ci: (bi, ci, gi)), state_spec],
        scratch_shapes=[pltpu.VMEM((gs, n, w), F32), pltpu.VMEM((gs, SUBLANES + q, w + 2 * n), F32)],
        compiler_params=_params(("parallel", "parallel", "arbitrary"), est),
        name="ssd_mixer",
    )(zx, zx, zx, zx, dtc, dtp, state0, conv_hist, conv_hist, conv_hist,
      conv_w, conv_w, conv_w, conv_b2, conv_b2, conv_b2,
      a_log.reshape(g, 1, r), pair_lanes(a_log), dt_bias.reshape(g, 1, r), pair_lanes(dt_bias),
      jnp.repeat(d_skip, p).reshape(g, 1, w), norm_g.reshape(g, 1, w))
    h_fin = state.reshape(b, g, n, r, p).transpose(0, 1, 3, 4, 2).reshape(b, heads, p, n)
    return ya, h_fin


def _gelu_tanh(x):
    return 0.5 * x * (1.0 + jnp.tanh(0.7978845608028654 * (x + 0.044715 * (x * x * x))))


def _s5_body(u_ref, h0re_ref, h0im_ref, lam_ref, bblk_ref, cblk_ref, d_ref,
             y_ref, ybf_ref, hre_ref, him_ref, bu_sc, h_sc, *, n_seq, n_pack, t_blk, pitch, n_tiles, unroll):
    j = pl.program_id(1)
    half = n_tiles // 2
    rows = n_seq * t_blk
    cw = u_ref.shape[-1] // n_pack
    sw = half * LANES

    @pl.when(j == 0)
    def _load_state():
        for pk in range(n_pack):
            for k in range(half):
                cols = slice(pk * sw + k * LANES, pk * sw + (k + 1) * LANES)
                h_sc[k, pk * n_seq:(pk + 1) * n_seq, :] = h0re_ref[:, cols]
                h_sc[half + k, pk * n_seq:(pk + 1) * n_seq, :] = h0im_ref[:, cols]

    for pk in range(n_pack):
        u_b = u_ref[:, :, pk * cw:(pk + 1) * cw].reshape(rows, cw).astype(BF16)
        for kk in range(half):
            bu = jnp.dot(u_b, bblk_ref[pk, :, kk * 2 * LANES:(kk + 1) * 2 * LANES], preferred_element_type=F32)
            for s in range(n_seq):
                base = (pk * n_seq + s) * pitch
                bu_sc[2 * kk, base:base + t_blk, :] = bu[s * t_blk:(s + 1) * t_blk, :LANES]
                bu_sc[2 * kk + 1, base:base + t_blk, :] = bu[s * t_blk:(s + 1) * t_blk, LANES:]

    def lam_rows(k):
        return jnp.concatenate([jnp.broadcast_to(lam_ref[pk, :, k * LANES:(k + 1) * LANES], (n_seq, LANES))
                                for pk in range(n_pack)], axis=0)

    lam_re = [lam_rows(k) for k in range(half)]
    lam_im = [lam_rows(half + k) for k in range(half)]

    def step(t, hs):
        idx = pl.ds(t, n_pack * n_seq, stride=pitch)
        new_re, new_im = [], []
        for k in range(half):
            h_re, h_im = hs[k], hs[half + k]
            re_rows, im_rows = bu_sc.at[k], bu_sc.at[half + k]
            n_re = lam_re[k] * h_re - lam_im[k] * h_im + re_rows[idx, :]
            n_im = lam_re[k] * h_im + lam_im[k] * h_re + im_rows[idx, :]
            re_rows[idx, :] = n_re
            im_rows[idx, :] = n_im
            new_re.append(n_re)
            new_im.append(n_im)
        return tuple(new_re + new_im)

    hs = lax.fori_loop(0, t_blk, step, tuple(h_sc[k] for k in range(n_tiles)), unroll=unroll)
    for k in range(n_tiles):
        h_sc[k] = hs[k]

    for pk in range(n_pack):
        h_all = jnp.concatenate(
            [jnp.concatenate([bu_sc[k, (pk * n_seq + s) * pitch:(pk * n_seq + s) * pitch + t_blk, :].astype(BF16)
                              for s in range(n_seq)], axis=0) for k in range(n_tiles)], axis=1)
        u = u_ref[:, :, pk * cw:(pk + 1) * cw].reshape(rows, cw)
        y = jnp.dot(h_all, cblk_ref[pk], preferred_element_type=F32) + d_ref[:, pk * cw:(pk + 1) * cw] * u
        y = _gelu_tanh(y)
        y_ref[:, :, pk * cw:(pk + 1) * cw] = y.reshape(n_seq, t_blk, cw)
        ybf_ref[:, :, pk * cw:(pk + 1) * cw] = y.astype(BF16).reshape(n_seq, t_blk, cw)

    @pl.when(j == pl.num_programs(1) - 1)
    def _store_state():
        for pk in range(n_pack):
            for k in range(half):
                cols = slice(pk * sw + k * LANES, pk * sw + (k + 1) * LANES)
                hre_ref[:, cols] = hs[k][pk * n_seq:(pk + 1) * n_seq, :]
                him_ref[:, cols] = hs[half + k][pk * n_seq:(pk + 1) * n_seq, :]


def _s5_discretize(lam_re, lam_im, log_step, b_re, b_im, c_re, c_im):
    g, p, k = b_re.shape
    gb = S5_GROUP_BLOCK
    nb = g // gb
    step = jnp.exp(log_step)[:, None]
    mag = jnp.exp(lam_re * step)
    lb_re, lb_im = mag * jnp.cos(lam_im * step), mag * jnp.sin(lam_im * step)
    den = lam_re * lam_re + lam_im * lam_im
    f_re = ((lb_re - 1.0) * lam_re + lb_im * lam_im) / den
    f_im = (lb_im * lam_re - (lb_re - 1.0) * lam_im) / den
    bb_re = f_re[..., None] * b_re - f_im[..., None] * b_im
    bb_im = f_re[..., None] * b_im + f_im[..., None] * b_re
    eye = jnp.eye(gb, dtype=F32)
    lam = jnp.stack([lb_re.reshape(nb, gb * p), lb_im.reshape(nb, gb * p)], axis=1).reshape(nb, 1, 2 * gb * p)
    bb = jnp.stack([bb_re, bb_im]).reshape(2, nb, gb, p, k).transpose(1, 2, 4, 0, 3)
    bblk = (bb[:, :, :, :, None, :] * eye[None, :, None, None, :, None]).reshape(nb, gb * k, 2 * gb * p)
    cc = jnp.stack([c_re, -c_im]).reshape(2, nb, gb, k, p).transpose(1, 0, 2, 4, 3)
    cblk = (cc[:, :, :, :, None, :] * eye[None, None, :, None, :, None]).reshape(nb, 2 * gb * p, gb * k)
    return lam, bblk.astype(BF16), cblk.astype(BF16)


def _s5_mixer(u, h0_re, h0_im, lam, bblk, cblk, d_skip):
    b, l, width = u.shape
    _, g, p = h0_re.shape
    nb, cw, sw2 = bblk.shape
    n_tiles = sw2 // LANES
    n_pack = max(1, SUBLANES // b)
    if nb % n_pack:
        n_pack = 1
    t_blk = _largest_divisor(l, (256, 128, 64, 32, 16, 8))
    rows = n_pack * b * t_blk
    pitch = SUBLANES * (t_blk // SUBLANES + (1 - (t_blk // SUBLANES) % 2))
    est = 6 * rows * sw2 + 2 * rows * cw * 10 + 8 * n_pack * cw * sw2 + 6 * rows * cw * 4
    body = functools.partial(_s5_body, n_seq=b, n_pack=n_pack, t_blk=t_blk, pitch=pitch, n_tiles=n_tiles,
                             unroll=S5_SCAN_UNROLL)
    seq_spec = pl.BlockSpec((b, t_blk, n_pack * cw), lambda gi, ti: (0, ti, gi))
    state_spec = pl.BlockSpec((b, n_pack * sw2 // 2), lambda gi, ti: (0, gi))
    y, ybf, h_re, h_im = pl.pallas_call(
        body,
        out_shape=[jax.ShapeDtypeStruct((b, l, width), F32), jax.ShapeDtypeStruct((b, l, width), BF16),
                   jax.ShapeDtypeStruct((b, g * p), F32), jax.ShapeDtypeStruct((b, g * p), F32)],
        grid=(nb // n_pack, l // t_blk),
        in_specs=[seq_spec, state_spec, state_spec,
                  pl.BlockSpec((n_pack, 1, sw2), lambda gi, ti: (gi, 0, 0)),
                  pl.BlockSpec((n_pack, cw, sw2), lambda gi, ti: (gi, 0, 0)),
                  pl.BlockSpec((n_pack, sw2, cw), lambda gi, ti: (gi, 0, 0)),
                  pl.BlockSpec((1, n_pack * cw), lambda gi, ti: (0, gi))],
        out_specs=[seq_spec, seq_spec, state_spec, state_spec],
        scratch_shapes=[pltpu.VMEM((n_tiles, n_pack * b * pitch, LANES), F32),
                        pltpu.VMEM((n_tiles, n_pack * b, LANES), F32)],
        compiler_params=_params(("parallel", "arbitrary"), est),
        name="s5_mixer",
    )(u, h0_re.reshape(b, g * p), h0_im.reshape(b, g * p), lam, bblk, cblk, d_skip.reshape(1, width))
    return y, ybf, h_re.reshape(b, g, p), h_im.reshape(b, g, p)


def _short_conv_body(b_ref, c_ref, v_ref, hist_ref, w_ref, o_ref, hnew_ref, cv_sc, *, t_blk, conv_k):
    t = pl.program_id(2)
    t0 = SUBLANES - (conv_k - 1)

    @pl.when(t == 0)
    def _load_hist():
        cv_sc[t0:SUBLANES, :] = hist_ref[...]

    cv_sc[SUBLANES:SUBLANES + t_blk, :] = c_ref[...] * v_ref[...]
    acc = cv_sc[t0:t0 + t_blk, :] * w_ref[0:1, :]
    for j in range(1, conv_k):
        acc = acc + cv_sc[t0 + j:t0 + j + t_blk, :] * w_ref[j:j + 1, :]
    o_ref[...] = (b_ref[...] * acc).astype(o_ref.dtype)
    cv_sc[t0:SUBLANES, :] = cv_sc[SUBLANES + t_blk - (conv_k - 1):SUBLANES + t_blk, :]

    @pl.when(t == pl.num_programs(2) - 1)
    def _store_hist():
        hnew_ref[...] = cv_sc[t0:SUBLANES, :]


def _short_conv(bcv, hist, conv_w):
    bsz, l, width3 = bcv.shape
    width = width3 // 3
    conv_k = conv_w.shape[0]
    t_blk = _largest_divisor(l, (256, 128, 64, 32, 16, 8))
    tw = _largest_divisor(width, (1024, 512, 256, 128))
    nw = width // tw

    def col(first):
        return pl.BlockSpec((None, t_blk, tw), lambda bi, wi, ti: (bi, ti, first + wi))

    body = functools.partial(_short_conv_body, t_blk=t_blk, conv_k=conv_k)
    return pl.pallas_call(
        body,
        out_shape=[jax.ShapeDtypeStruct((bsz, l, width), BF16), jax.ShapeDtypeStruct(hist.shape, F32)],
        grid=(bsz, nw, l // t_blk),
        in_specs=[col(0), col(nw), col(2 * nw),
                  pl.BlockSpec((None, conv_k - 1, tw), lambda bi, wi, ti: (bi, 0, wi)),
                  pl.BlockSpec((conv_k, tw), lambda bi, wi, ti: (0, wi))],
        out_specs=[pl.BlockSpec((None, t_blk, tw), lambda bi, wi, ti: (bi, ti, wi)),
                   pl.BlockSpec((None, conv_k - 1, tw), lambda bi, wi, ti: (bi, 0, wi))],
        scratch_shapes=[pltpu.VMEM((SUBLANES + t_blk, tw), F32)],
        compiler_params=_params(("parallel", "parallel", "arbitrary"), 4 * t_blk * tw * 12),
        name="short_conv",
    )(bcv, bcv, bcv, hist, conv_w)


def _prepare_weights(w):
    heads, p, _ = w["ssd_shape"]
    inner = heads * p
    conv_dim = w["ssd_conv_w"].shape[-1]
    out = {}
    assert w["w_out0"].shape[1] == 2 * inner
    layers, _, in0_cols = w["w_in0"].shape
    u_start = inner + conv_dim + heads
    w_in0 = jnp.pad(w["w_in0"].astype(BF16), ((0, 0), (0, 0), (0, _round_up(in0_cols, 4 * LANES) - in0_cols)))
    out["w_zx"] = {j: _LayerWeight(w_in0, j, n=inner + conv_dim) for j in range(layers)}
    out["w_dt"] = {j: _LayerWeight(w_in0, j, col_start=inner + conv_dim, n=_round_up(heads, LANES)) for j in range(layers)}
    out["w_u"] = {j: _shifted_columns(w_in0, j, u_start, in0_cols - u_start) for j in range(layers)}
    out["s5"] = [_s5_discretize(w["s5_lam_re"][j], w["s5_lam_im"][j], w["s5_log_step"][j], w["s5_b_re"][j],
                                w["s5_b_im"][j], w["s5_c_re"][j], w["s5_c_im"][j])
                 for j in range(w["s5_lam_re"].shape[0])]
    return out


def _run_trunk(x3, p4, conv_ssd, h_ssd, s5_re, s5_im, conv_sc, w, wb, emitted):
    bsz, l, d = x3.shape
    m = bsz * l
    depth = w["g_ffn1"].shape[0]
    heads = h_ssd.shape[2]
    x = x3.reshape(m, d)
    inner = heads * h_ssd.shape[3]
    conv_in_projection = emitted is None and l % CONV_MM_ROWS == 0

    def f32_window(name, i):
        if name in ("w_out0_a", "w_out0_b"):
            return _LayerWeight(w["w_out0"], i, k=inner, row_block=0 if name == "w_out0_a" else 1)
        return _LayerWeight(w[name], i)

    def mm(a_list, names, i, dots, extras, epilogue, out_dtype, tag):
        if names[0] in wb:
            ws = [wb[nm][i] if isinstance(wb[nm], dict) else _LayerWeight(wb[nm], i) for nm in names]
            return _fused_matmul(a_list, ws, dots, extras, epilogue, out_dtype, tag)
        out, copies = _fused_matmul(a_list, [f32_window(nm, i) for nm in names], dots, extras, epilogue,
                                    out_dtype, tag, emit_weights=True)
        for nm, copy in zip(names, copies):
            emitted.setdefault(nm, {})[i] = copy
        return out

    def ffn_half_step(x, g, i, tag):
        xn = _rmsnorm(x, g, BF16)
        h = mm([xn], [f"w_{tag}_gate", f"w_{tag}_up"], i, [(0, 0), (0, 1)], [], _ep_swiglu, BF16, f"{tag}_gate_up")
        return mm([h], [f"w_{tag}_down"], i, [(0, 0)], [x], _ep_half_residual, F32, f"{tag}_down")

    out_conv_ssd, out_h, out_re, out_im, out_sc = [], [], [], [], []
    for i in range(depth):
        x = ffn_half_step(x, w["g_ffn1"][i], i, "ffn1")
        xn = _rmsnorm(x, w["g_mix"][i], BF16)
        j = i // 2
        if i % 2 == 0:
            conv_k, conv_dim = w["ssd_conv_w"][j].shape
            zx = mm([xn], ["w_zx"], j, [(0, 0)], [], _ep_plain, F32, "in0_zxbc")
            dt = mm([xn], ["w_dt"], j, [(0, 0)], [], _ep_plain, F32, "in0_dt")[:, :heads]
            u = mm([xn], ["w_u"], j, [(0, 0)], [], _ep_plain, F32, "in0_u")
            zx3 = zx.reshape(bsz, l, zx.shape[-1])
            ya, h_new = _ssd_mixer(zx3, dt.reshape(bsz, l, heads), conv_ssd[j], h_ssd[j], w["ssd_conv_w"][j],
                                   w["ssd_conv_b"][j], w["ssd_a_log"][j], w["ssd_dt_bias"][j], w["ssd_d"][j],
                                   w["ssd_norm_g"][j])
            xbc_tail = zx3[:, l - min(l, conv_k - 1):, inner:inner + conv_dim]
            c_new = jnp.concatenate([conv_ssd[j], xbc_tail], axis=1)[:, -(conv_k - 1):]
            lam, bblk, cblk = wb["s5"][j]
            yb, yb_bf, re_new, im_new = _s5_mixer(u.reshape(bsz, l, u.shape[-1]), s5_re[j], s5_im[j], lam, bblk, cblk,
                                                  w["s5_d"][j])
            width = yb.shape[-1]
            yb_glu = mm([yb_bf.reshape(m, width)], ["s5_w_glu"], j, [(0, 0)],
                        [yb.reshape(m, width), w["s5_b_glu"][j].reshape(1, width)], _ep_glu, BF16, "s5_glu")
            x = mm([ya.reshape(m, inner), yb_glu], ["w_out0_a", "w_out0_b"], j, [(0, 0), (1, 1)], [x],
                   _ep_residual2, F32, "out0")
            out_conv_ssd.append(c_new)
            out_h.append(h_new)
            out_re.append(re_new)
            out_im.append(im_new)
        else:
            if conv_in_projection:
                taps = w["sc_conv_w"][j]
                yc, tail = _short_conv_projection(xn, wb["w_in1"][j], conv_sc[j], taps, l)
                sc_new = tail[:, SUBLANES - (taps.shape[0] - 1):, :]
            else:
                bcv = mm([xn], ["w_in1"], j, [(0, 0)], [], _ep_plain, F32, "in1")
                yc, sc_new = _short_conv(bcv.reshape(bsz, l, bcv.shape[-1]), conv_sc[j], w["sc_conv_w"][j])
            x = mm([yc.reshape(m, yc.shape[-1])], ["w_out1"], j, [(0, 0)], [x], _ep_residual, F32, "out1")
            out_sc.append(sc_new)
        x = ffn_half_step(x, w["g_ffn2"][i], i, "ffn2")
        xn = _rmsnorm(x, w["g_pe"][i], BF16)
        x = mm([xn, p4[i].reshape(m, p4.shape[-1])], ["w_pe_gate", "w_pe_proj"], i, [(0, 0), (1, 1)], [x],
               _ep_gated_embed, F32, "pe_embed")
    y = _rmsnorm(x, w["g_final"], F32).reshape(bsz, l, d)
    return (y, jnp.stack(out_conv_ssd), jnp.stack(out_h), jnp.stack(out_re), jnp.stack(out_im), jnp.stack(out_sc))


def kernel(x_prompt, x_sample, p_prompt, p_sample, cache_conv_ssd, state_ssd, state_s5_re, state_s5_im, cache_conv_short, g_ffn1, w_ffn1_gate, w_ffn1_up, w_ffn1_down, g_mix, g_ffn2, w_ffn2_gate, w_ffn2_up, w_ffn2_down, g_pe, w_pe_gate, w_pe_proj, w_in0, ssd_conv_w, ssd_conv_b, ssd_a_log, ssd_dt_bias, ssd_d, ssd_norm_g, s5_lam_re, s5_lam_im, s5_log_step, s5_b_re, s5_b_im, s5_c_re, s5_c_im, s5_d, s5_w_glu, s5_b_glu, w_out0, w_in1, sc_conv_w, w_out1, g_final):
    w = dict(g_ffn1=g_ffn1, w_ffn1_gate=w_ffn1_gate, w_ffn1_up=w_ffn1_up, w_ffn1_down=w_ffn1_down,
             g_mix=g_mix, g_ffn2=g_ffn2, w_ffn2_gate=w_ffn2_gate, w_ffn2_up=w_ffn2_up, w_ffn2_down=w_ffn2_down,
             g_pe=g_pe, w_pe_gate=w_pe_gate, w_pe_proj=w_pe_proj, w_in0=w_in0, ssd_conv_w=ssd_conv_w,
             ssd_conv_b=ssd_conv_b, ssd_a_log=ssd_a_log, ssd_dt_bias=ssd_dt_bias, ssd_d=ssd_d,
             ssd_norm_g=ssd_norm_g, s5_lam_re=s5_lam_re, s5_lam_im=s5_lam_im, s5_log_step=s5_log_step,
             s5_b_re=s5_b_re, s5_b_im=s5_b_im, s5_c_re=s5_c_re, s5_c_im=s5_c_im, s5_d=s5_d,
             s5_w_glu=s5_w_glu, s5_b_glu=s5_b_glu, w_out0=w_out0, w_in1=w_in1, sc_conv_w=sc_conv_w,
             w_out1=w_out1, g_final=g_final, ssd_shape=state_ssd.shape[2:])
    wb = _prepare_weights(w)
    emitted = {}
    sample = _run_trunk(x_sample, p_sample, cache_conv_ssd, state_ssd, state_s5_re, state_s5_im, cache_conv_short, w,
                        wb, emitted)
    wb.update(emitted)
    b = x_prompt.shape[0]
    zeros = lambda ref: jnp.zeros((ref.shape[0], b) + ref.shape[2:], F32)
    prompt = _run_trunk(x_prompt, p_prompt, zeros(cache_conv_ssd), zeros(state_ssd), zeros(state_s5_re),
                        zeros(state_s5_im), zeros(cache_conv_short), w, wb, None)
    return (prompt[0], sample[0]) + prompt[1:] + sample[1:]
```
